```python
import math
import jax, jax.numpy as jnp
from jax import lax
import numpy as np

D_MODEL = 1024
BATCH = 4
SEQ = 8192
DEPTH = 2

N_MIXERS = 2
N_CONV = (DEPTH + 1) // 2
N_ATTN = DEPTH // 2
CONV_WIDTH = 31
N_HEADS = 8
HEAD_DIM = D_MODEL // (2 * N_HEADS)
V_DIM = 2 * HEAD_DIM
Q_BLOCK = 128
NUM_BUCKETS = 32
MAX_EXACT = NUM_BUCKETS // 2
MAX_DISTANCE = 128
D_FF = int(math.ceil(8 * D_MODEL / 3 / 256) * 256)
PLE_DIM = 256
EPS = 1e-6

kernel_name = "hybrid_conformer_diffattn_trunk"


def rms_norm(x, g):
    xf = x.astype(jnp.float32)
    y = xf * lax.rsqrt(jnp.mean(xf * xf, axis=-1, keepdims=True) + EPS)
    return (y * g.astype(jnp.float32)).astype(x.dtype)


def layer_norm(x, g, b):
    xf = x.astype(jnp.float32)
    mu = jnp.mean(xf, axis=-1, keepdims=True)
    xc = xf - mu
    var = jnp.mean(xc * xc, axis=-1, keepdims=True)
    y = xc * lax.rsqrt(var + EPS) * g.astype(jnp.float32) + b.astype(jnp.float32)
    return y.astype(x.dtype)


def conformer_conv(x, w_pw1, b_pw1, dw_w, dw_b, ln_g, ln_b, w_pw2, b_pw2):
    a = x @ w_pw1 + b_pw1
    u = a[..., :D_MODEL] * jax.nn.sigmoid(a[..., D_MODEL:])
    u = lax.conv_general_dilated(
        u, dw_w[:, None, :].astype(u.dtype), window_strides=(1,),
        padding=[(CONV_WIDTH - 1, 0)],
        dimension_numbers=("NWC", "WIO", "NWC"),
        feature_group_count=D_MODEL) + dw_b
    u = jax.nn.silu(layer_norm(u, ln_g, ln_b))
    return u @ w_pw2 + b_pw2


def t5_bucket(dist):
    n = jnp.maximum(dist, 0)
    is_small = n < MAX_EXACT
    nf = jnp.maximum(n, 1).astype(jnp.float32)
    large = MAX_EXACT + (jnp.log(nf / MAX_EXACT) / math.log(MAX_DISTANCE / MAX_EXACT)
                         * (NUM_BUCKETS - MAX_EXACT)).astype(jnp.int32)
    large = jnp.minimum(large, NUM_BUCKETS - 1)
    return jnp.where(is_small, n, large)


def diff_attention(x, w_qkv, q_g, k_g, lq1, lk1, lq2, lk2, sub_g, w_o, rel_bias, lambda_init):
    B, S, _ = x.shape
    qkv = x @ w_qkv
    q, k, v = jnp.split(qkv, 3, axis=-1)
    q = rms_norm(q.reshape(B, S, N_HEADS, 2, HEAD_DIM), q_g) * (HEAD_DIM ** -0.5)
    k = rms_norm(k.reshape(B, S, N_HEADS, 2, HEAD_DIM), k_g)
    v = v.reshape(B, S, N_HEADS, V_DIM)
    q = jnp.transpose(q, (0, 2, 3, 1, 4))
    k = jnp.transpose(k, (0, 2, 3, 1, 4))
    v = jnp.transpose(v, (0, 2, 1, 3))
    lam = (jnp.exp(jnp.sum(lq1.astype(jnp.float32) * lk1.astype(jnp.float32)))
           - jnp.exp(jnp.sum(lq2.astype(jnp.float32) * lk2.astype(jnp.float32)))
           + lambda_init)
    n_blk = S // Q_BLOCK
    qb = jnp.moveaxis(q.reshape(B, N_HEADS, 2, n_blk, Q_BLOCK, HEAD_DIM), 3, 0)
    k_pos = jnp.arange(S, dtype=jnp.int32)

    def block(args):
        q_blk, bi = args
        q_pos = bi * Q_BLOCK + jnp.arange(Q_BLOCK, dtype=jnp.int32)
        dist = q_pos[:, None] - k_pos[None, :]
        bias = jnp.transpose(rel_bias[t5_bucket(dist)], (2, 0, 1)).astype(jnp.float32)
        logits = jnp.einsum('bhcqd,bhckd->bhcqk', q_blk, k).astype(jnp.float32)
        logits = logits + bias[None, :, None]
        logits = jnp.where(dist >= 0, logits, -jnp.inf)
        probs = jax.nn.softmax(logits, axis=-1)
        w = probs[:, :, 0] - lam * probs[:, :, 1]
        return jnp.einsum('bhqk,bhkd->bhqd', w.astype(v.dtype), v)

    outs = lax.map(block, (qb, jnp.arange(n_blk, dtype=jnp.int32)))
    o = jnp.transpose(outs, (1, 0, 3, 2, 4)).reshape(B, S, N_HEADS, V_DIM)
    o = rms_norm(o, sub_g) * (1.0 - lambda_init)
    return o.reshape(B, S, N_HEADS * V_DIM) @ w_o


def swiglu(x, w_gate, w_up, w_down):
    return (jax.nn.silu(x @ w_gate) * (x @ w_up)) @ w_down


def setup_inputs(seed: int = 0) -> dict:
    key = jax.random.key(seed)
    ks = iter(jax.random.split(key, 64))
    f32 = jnp.float32

    def nrm(shape, scale):
        return jax.random.normal(next(ks), shape, f32) * scale

    def gain(shape):
        return 1.0 + nrm(shape, 0.05)

    D = D_MODEL
    return {
        "x": nrm((BATCH, SEQ, D), 1.0),
        "p": nrm((DEPTH, BATCH, SEQ, PLE_DIM), 1.0),
        "conv_norm_g": gain((N_CONV, D)),
        "conv_w_pw1": nrm((N_CONV, D, 2 * D), D ** -0.5),
        "conv_b_pw1": nrm((N_CONV, 2 * D), 0.02),
        "conv_dw_w": nrm((N_CONV, CONV_WIDTH, D), CONV_WIDTH ** -0.5),
        "conv_dw_b": nrm((N_CONV, D), 0.02),
        "conv_ln_g": gain((N_CONV, D)),
        "conv_ln_b": nrm((N_CONV, D), 0.02),
        "conv_w_pw2": nrm((N_CONV, D, D), D ** -0.5),
        "conv_b_pw2": nrm((N_CONV, D), 0.02),
        "attn_norm_g": gain((N_ATTN, D)),
        "attn_w_qkv": nrm((N_ATTN, D, 3 * D), D ** -0.5),
        "attn_q_norm_g": gain((N_ATTN, HEAD_DIM)),
        "attn_k_norm_g": gain((N_ATTN, HEAD_DIM)),
        "attn_lambda_q1": nrm((N_ATTN, HEAD_DIM), 0.1),
        "attn_lambda_k1": nrm((N_ATTN, HEAD_DIM), 0.1),
        "attn_lambda_q2": nrm((N_ATTN, HEAD_DIM), 0.1),
        "attn_lambda_k2": nrm((N_ATTN, HEAD_DIM), 0.1),
        "attn_sub_norm_g": gain((N_ATTN, V_DIM)),
        "attn_w_o": nrm((N_ATTN, D, D), D ** -0.5),
        "rel_bias": nrm((NUM_BUCKETS, N_HEADS), 0.5),
        "ffn_norm_g": gain((DEPTH, D)),
        "ffn_w_gate": nrm((DEPTH, D, D_FF), D ** -0.5),
        "ffn_w_up": nrm((DEPTH, D, D_FF), D ** -0.5),
        "ffn_w_down": nrm((DEPTH, D_FF, D), D_FF ** -0.5),
        "ple_norm_g": gain((DEPTH, D)),
        "ple_w_gate": nrm((DEPTH, D, D), D ** -0.5),
        "ple_w_proj": nrm((DEPTH, PLE_DIM, D), PLE_DIM ** -0.5),
    }


def reference(x, p, conv_norm_g, conv_w_pw1, conv_b_pw1, conv_dw_w, conv_dw_b, conv_ln_g,
              conv_ln_b, conv_w_pw2, conv_b_pw2, attn_norm_g, attn_w_qkv, attn_q_norm_g,
              attn_k_norm_g, attn_lambda_q1, attn_lambda_k1, attn_lambda_q2, attn_lambda_k2,
              attn_sub_norm_g, attn_w_o, rel_bias, ffn_norm_g, ffn_w_gate, ffn_w_up, ffn_w_down,
              ple_norm_g, ple_w_gate, ple_w_proj):
    h = x
    for i in range(DEPTH):
        j = i // N_MIXERS
        if i % N_MIXERS == 0:
            u = rms_norm(h, conv_norm_g[j])
            h = h + conformer_conv(u, conv_w_pw1[j], conv_b_pw1[j], conv_dw_w[j], conv_dw_b[j],
                                   conv_ln_g[j], conv_ln_b[j], conv_w_pw2[j], conv_b_pw2[j])
        else:
            lambda_init = 0.8 - 0.6 * math.exp(-0.3 * i)
            u = rms_norm(h, attn_norm_g[j])
            h = h + diff_attention(u, attn_w_qkv[j], attn_q_norm_g[j], attn_k_norm_g[j],
                                   attn_lambda_q1[j], attn_lambda_k1[j], attn_lambda_q2[j],
                                   attn_lambda_k2[j], attn_sub_norm_g[j], attn_w_o[j],
                                   rel_bias, lambda_init)
        u = rms_norm(h, ffn_norm_g[i])
        h = h + swiglu(u, ffn_w_gate[i], ffn_w_up[i], ffn_w_down[i])
        gate = jax.nn.sigmoid(rms_norm(h, ple_norm_g[i]) @ ple_w_gate[i])
        h = h + gate * (p[i] @ ple_w_proj[i])
    return h
```

```python
import functools
import math

import jax
import jax.numpy as jnp
from jax import lax
from jax.experimental import pallas as pl
from jax.experimental.pallas import tpu as pltpu

D_MODEL = 1024
CONV_WIDTH = 31
N_HEADS = 8
HEAD_DIM = 64
V_DIM = 128
NUM_BUCKETS = 32
MAX_EXACT = 16
MAX_DISTANCE = 128
PLE_DIM = 256
EPS = 1e-6

LANES = 128
MXU_DIM = 256
VMEM_BYTES = 64 * 2 ** 20
HALO = 32
NEG_BIG = -1e30

F32 = jnp.float32
BF16 = jnp.bfloat16


def _rms(x, g):
    return x * lax.rsqrt(jnp.mean(x * x, axis=-1, keepdims=True) + EPS) * g


def _dot(a, b):
    return jnp.dot(a, b, preferred_element_type=F32)


def _const_spec(shape):
    nd = len(shape)
    return pl.BlockSpec(shape, lambda *_: (0,) * nd, pipeline_mode=pl.Buffered(1))


def _params(semantics, vmem_mib):
    return pltpu.CompilerParams(dimension_semantics=semantics,
                                vmem_limit_bytes=min(vmem_mib * 2 ** 20, VMEM_BYTES - 6 * 2 ** 20))


def _conv_mixer_kernel(x_ref, ng_ref, w1_ref, b1_ref, dw_ref, dwb_ref, lng_ref, lnb_ref,
                       w2_ref, b2_ref, o_ref, ubuf_ref, ybuf_ref, *, tm, rc):
    d = D_MODEL
    nc = d // LANES

    @pl.when(pl.program_id(1) == 0)
    def _():
        ubuf_ref[:, 0:HALO, :] = jnp.zeros((nc, HALO, LANES), F32)

    x = x_ref[0]
    xn = _rms(x, ng_ref[...]).astype(BF16)
    a = _dot(xn, w1_ref[...]) + b1_ref[...]
    u = a[:, :d] * jax.nn.sigmoid(a[:, d:])
    for c in range(nc):
        ubuf_ref[c, HALO:HALO + tm, :] = u[:, c * LANES:(c + 1) * LANES]

    def col_body(c, carry):
        for r0 in range(0, tm, rc):
            acc = None
            for k in range(CONV_WIDTH):
                seg = ubuf_ref[c, pl.ds(r0 + k + HALO - (CONV_WIDTH - 1), rc), :]
                t = seg * dw_ref[c, k:k + 1, :]
                acc = t if acc is None else acc + t
            ybuf_ref[c, r0:r0 + rc, :] = acc
        ubuf_ref[c, 0:HALO, :] = ubuf_ref[c, tm:tm + HALO, :]
        return carry

    lax.fori_loop(0, nc, col_body, 0)

    y = jnp.concatenate([ybuf_ref[c] for c in range(nc)], axis=-1) + dwb_ref[...]
    mu = jnp.mean(y, axis=-1, keepdims=True)
    yc = y - mu
    var = jnp.mean(yc * yc, axis=-1, keepdims=True)
    yn = yc * lax.rsqrt(var + EPS) * lng_ref[...] + lnb_ref[...]
    z = (yn * jax.nn.sigmoid(yn)).astype(BF16)
    o_ref[0] = x + _dot(z, w2_ref[...]) + b2_ref[...]


def _conv_mixer(x, ng, w1, b1, dw, dwb, lng, lnb, w2, b2, *, tm=256, rc=64):
    b, s, d = x.shape
    nc = d // LANES
    dw_p = jnp.pad(dw, ((0, HALO - CONV_WIDTH), (0, 0))).reshape(HALO, nc, LANES).transpose(1, 0, 2)
    row = lambda v: v.reshape(1, -1).astype(F32)
    kern = functools.partial(_conv_mixer_kernel, tm=tm, rc=rc)
    return pl.pallas_call(
        kern,
        out_shape=jax.ShapeDtypeStruct((b, s, d), F32),
        grid=(b, s // tm),
        in_specs=[
            pl.BlockSpec((1, tm, d), lambda i, j: (i, j, 0)),
            _const_spec((1, d)), _const_spec((d, 2 * d)), _const_spec((1, 2 * d)),
            _const_spec((nc, HALO, LANES)), _const_spec((1, d)), _const_spec((1, d)),
            _const_spec((1, d)), _const_spec((d, d)), _const_spec((1, d)),
        ],
        out_specs=pl.BlockSpec((1, tm, d), lambda i, j: (i, j, 0)),
        scratch_shapes=[pltpu.VMEM((nc, HALO + tm, LANES), F32), pltpu.VMEM((nc, tm, LANES), F32)],
        compiler_params=_params(("arbitrary", "arbitrary"), 40),
        name="conv_mixer",
    )(x, row(ng), w1.astype(BF16), row(b1), dw_p, row(dwb), row(lng), row(lnb), w2.astype(BF16), row(b2))


def _ffn_chunks(d_ff):
    step = 2 * MXU_DIM
    return [(c0, min(c0 + step, d_ff)) for c0 in range(0, d_ff, step)]


def _ffn_ple_kernel(*refs, has_pre):
    if has_pre:
        (h_ref, on_ref, wo_ref, fg_ref, wg_ref, wu_ref, wd_ref, p_ref, pg_ref, wpg_ref, wpp_ref,
         o_ref) = refs
        h = h_ref[...] + _dot(on_ref[...], wo_ref[...])
    else:
        (h_ref, fg_ref, wg_ref, wu_ref, wd_ref, p_ref, pg_ref, wpg_ref, wpp_ref, o_ref) = refs
        h = h_ref[...]
    xn = _rms(h, fg_ref[...]).astype(BF16)
    acc = None
    for c0, c1 in _ffn_chunks(wg_ref.shape[1]):
        g = _dot(xn, wg_ref[:, c0:c1])
        u = _dot(xn, wu_ref[:, c0:c1])
        a = (g * jax.nn.sigmoid(g) * u).astype(BF16)
        t = _dot(a, wd_ref[c0:c1, :])
        acc = t if acc is None else acc + t
    h2 = h + acc
    gate = jax.nn.sigmoid(_dot(_rms(h2, pg_ref[...]).astype(BF16), wpg_ref[...]))
    proj = _dot(p_ref[...].astype(BF16), wpp_ref[...])
    o_ref[...] = h2 + gate * proj


def _ffn_ple(h, p, layer, fg, wg, wu, wd, pg, wpg, wpp, pre=None, *, tm=512):
    n, d = h.shape
    p_blk = pl.BlockSpec((tm, PLE_DIM), lambda i: (i + layer * (n // tm), 0))
    d_ff = wg.shape[1]
    row = lambda v: v.reshape(1, -1).astype(F32)
    tok = lambda w: pl.BlockSpec((tm, w), lambda i: (i, 0))
    args, specs = [h], [tok(d)]
    if pre is not None:
        on, wo = pre
        args += [on, wo.astype(BF16)]
        specs += [tok(d), _const_spec((d, d))]
    args += [row(fg), wg.astype(BF16), wu.astype(BF16), wd.astype(BF16), p, row(pg),
             wpg.astype(BF16), wpp.astype(BF16)]
    specs += [_const_spec((1, d)), _const_spec((d, d_ff)), _const_spec((d, d_ff)),
              _const_spec((d_ff, d)), p_blk, _const_spec((1, d)), _const_spec((d, d)),
              _const_spec((PLE_DIM, d))]
    return pl.pallas_call(
        functools.partial(_ffn_ple_kernel, has_pre=pre is not None),
        out_shape=jax.ShapeDtypeStruct((n, d), F32),
        grid=(n // tm,),
        in_specs=specs,
        out_specs=tok(d),
        compiler_params=_params(("parallel",), 56),
        name="ffn_ple_pre" if pre is not None else "ffn_ple",
    )(*args)


def _qkv_kernel(h_ref, ng_ref, w_ref, bd_ref, qg_ref, kg_ref, q_ref, k_ref, v_ref):
    d = D_MODEL
    xn = _rms(h_ref[...], ng_ref[...]).astype(BF16)
    qkv = _dot(xn, w_ref[...])

    def comp_norm(t, gain):
        sq = (t * t).astype(BF16)
        ms = jnp.concatenate([_dot(sq[:, c:c + MXU_DIM], bd_ref[...]) for c in range(0, d, MXU_DIM)],
                             axis=-1)
        return t * lax.rsqrt(ms + EPS) * gain

    q_ref[...] = comp_norm(qkv[:, :d], qg_ref[...]).astype(BF16)
    k_ref[...] = comp_norm(qkv[:, d:2 * d], kg_ref[...]).astype(BF16)
    v_ref[...] = qkv[:, 2 * d:].astype(BF16)


def _qkv_proj(h, ng, w, qg, kg, *, tm=512):
    n, d = h.shape
    reps = d // HEAD_DIM
    grp = jnp.arange(MXU_DIM) // HEAD_DIM
    bd = jnp.where(grp[:, None] == grp[None, :], 1.0 / HEAD_DIM, 0.0).astype(BF16)
    qg_t = (jnp.tile(qg.astype(F32), reps) * (HEAD_DIM ** -0.5)).reshape(1, d)
    kg_t = jnp.tile(kg.astype(F32), reps).reshape(1, d)
    tok = pl.BlockSpec((tm, d), lambda i: (i, 0))
    out = jax.ShapeDtypeStruct((n, d), BF16)
    return pl.pallas_call(
        _qkv_kernel,
        out_shape=(out, out, out),
        grid=(n // tm,),
        in_specs=[tok, _const_spec((1, d)), _const_spec((d, 3 * d)), _const_spec((MXU_DIM, MXU_DIM)),
                  _const_spec((1, d)), _const_spec((1, d))],
        out_specs=(tok, tok, tok),
        compiler_params=_params(("parallel",), 48),
        name="qkv_proj",
    )(h, ng.reshape(1, d).astype(F32), w.astype(BF16), bd, qg_t, kg_t)


def _attn_prep_kernel(rb_ref, lq1_ref, lk1_ref, lq2_ref, lk2_ref, d0_ref, d1_ref, lam_ref,
                      *, t, lambda_init):
    h = pl.program_id(0)
    far = rb_ref[NUM_BUCKETS - 1, h]
    dist = (lax.broadcasted_iota(jnp.int32, (t, t), 0) - lax.broadcasted_iota(jnp.int32, (t, t), 1))

    def bias_of(dd):
        n = jnp.maximum(dd, 0)
        nf = jnp.maximum(n, 1).astype(F32)
        large = MAX_EXACT + (jnp.log(nf / MAX_EXACT) / math.log(MAX_DISTANCE / MAX_EXACT)
                             * (NUM_BUCKETS - MAX_EXACT)).astype(jnp.int32)
        large = jnp.minimum(large, NUM_BUCKETS - 1)
        bucket = jnp.where(n < MAX_EXACT, n, large)
        out = jnp.zeros(dd.shape, F32)
        for bk in range(NUM_BUCKETS):
            out = jnp.where(bucket == bk, rb_ref[bk, h], out)
        return out - far

    d0_ref[0] = jnp.where(dist >= 0, bias_of(dist), NEG_BIG)
    d1_ref[0] = bias_of(dist + t)
    lam = (jnp.exp(jnp.sum(lq1_ref[...] * lk1_ref[...])) - jnp.exp(jnp.sum(lq2_ref[...] * lk2_ref[...]))
           + lambda_init)
    lam_ref[...] = jnp.full(lam_ref.shape, lam, F32)


def _attn_prep(rel_bias, lq1, lk1, lq2, lk2, *, t, lambda_init):
    row = lambda v: v.reshape(1, -1).astype(F32)
    vec = _const_spec((1, HEAD_DIM))
    tab = jax.ShapeDtypeStruct((N_HEADS, t, t), F32)
    return pl.pallas_call(
        functools.partial(_attn_prep_kernel, t=t, lambda_init=lambda_init),
        out_shape=(tab, tab, jax.ShapeDtypeStruct((8, LANES), F32)),
        grid=(N_HEADS,),
        in_specs=[pl.BlockSpec(memory_space=pltpu.SMEM), vec, vec, vec, vec],
        out_specs=(pl.BlockSpec((1, t, t), lambda h: (h, 0, 0)), pl.BlockSpec((1, t, t), lambda h: (h, 0, 0)),
                   pl.BlockSpec((8, LANES), lambda h: (0, 0))),
        compiler_params=_params(("arbitrary",), 32),
        name="attn_prep",
    )(rel_bias.astype(F32), row(lq1), row(lk1), row(lq2), row(lk2))


def _attn_kernel(q_ref, k_ref, v_ref, d0_ref, d1_ref, lam_ref, sg_ref, o_ref,
                 qst_ref, m_ref, l_ref, acc_ref, *, t, out_scale):
    qi = pl.program_id(2)
    reps = t // LANES

    q = q_ref[0]
    lane = lax.broadcasted_iota(jnp.int32, q.shape, 1)
    zero = jnp.zeros_like(q)
    qst_ref[0:t, :] = jnp.where(lane < HEAD_DIM, q, zero)
    qst_ref[t:2 * t, :] = jnp.where(lane >= HEAD_DIM, q, zero)
    m_ref[...] = jnp.full(m_ref.shape, NEG_BIG, F32)
    l_ref[...] = jnp.zeros(l_ref.shape, F32)
    acc_ref[...] = jnp.zeros(acc_ref.shape, F32)

    def step(j, bias_ref):
        start = pl.multiple_of(j * t, t)
        k = k_ref[0, pl.ds(start, t), :]
        v = v_ref[0, pl.ds(start, t), :]
        s = lax.dot_general(qst_ref[...], k, (((1,), (1,)), ((), ())), preferred_element_type=F32)
        if bias_ref is not None:
            bias = bias_ref[0]
            s = s + jnp.concatenate([bias, bias], axis=0)
        m_old = m_ref[...]
        m_new = jnp.maximum(m_old, jnp.max(s, axis=-1, keepdims=True))
        alpha = jnp.exp(m_old - m_new)
        p = jnp.exp(s - jnp.concatenate([m_new] * reps, axis=1))
        l_ref[...] = alpha * l_ref[...] + jnp.sum(p, axis=-1, keepdims=True)
        acc_ref[...] = alpha * acc_ref[...] + _dot(p.astype(BF16), v)
        m_ref[...] = m_new

    def plain_body(j, carry):
        step(j, None)
        return carry

    lax.fori_loop(0, jnp.maximum(qi - 1, 0), plain_body, 0)

    @pl.when(qi >= 1)
    def _():
        step(qi - 1, d1_ref)

    step(qi, d0_ref)

    o_all = acc_ref[...] / l_ref[...]
    lam = lam_ref[0:1, :]
    o = o_all[0:t] - lam * o_all[t:2 * t]
    o_ref[0] = (_rms(o, sg_ref[...]) * out_scale).astype(BF16)


def _attention(q, k, v, d0, d1, lam, sub_g, *, t, out_scale):
    b, s, d = q.shape
    qblk = pl.BlockSpec((1, t, V_DIM), lambda bi, h, i: (bi, i, h))
    kvblk = pl.BlockSpec((1, s, V_DIM), lambda bi, h, i: (bi, 0, h))
    tab = pl.BlockSpec((1, t, t), lambda bi, h, i: (h, 0, 0))
    return pl.pallas_call(
        functools.partial(_attn_kernel, t=t, out_scale=out_scale),
        out_shape=jax.ShapeDtypeStruct((b, s, d), BF16),
        grid=(b, N_HEADS, s // t),
        in_specs=[qblk, kvblk, kvblk, tab, tab, _const_spec((8, LANES)), _const_spec((1, V_DIM))],
        out_specs=qblk,
        scratch_shapes=[pltpu.VMEM((2 * t, V_DIM), BF16), pltpu.VMEM((2 * t, LANES), F32),
                        pltpu.VMEM((2 * t, LANES), F32), pltpu.VMEM((2 * t, V_DIM), F32)],
        compiler_params=_params(("parallel", "parallel", "arbitrary"), 40),
        name="diff_attention",
    )(q, k, v, d0, d1, lam, sub_g.reshape(1, V_DIM).astype(F32))


def kernel(x, p, conv_norm_g, conv_w_pw1, conv_b_pw1, conv_dw_w, conv_dw_b, conv_ln_g, conv_ln_b, conv_w_pw2, conv_b_pw2, attn_norm_g, attn_w_qkv, attn_q_norm_g, attn_k_norm_g, attn_lambda_q1, attn_lambda_k1, attn_lambda_q2, attn_lambda_k2, attn_sub_norm_g, attn_w_o, rel_bias, ffn_norm_g, ffn_w_gate, ffn_w_up, ffn_w_down, ple_norm_g, ple_w_gate, ple_w_proj):
    b, s, d = x.shape
    n = b * s
    t_attn = 256

    h = _conv_mixer(x, conv_norm_g[0], conv_w_pw1[0], conv_b_pw1[0], conv_dw_w[0], conv_dw_b[0],
                    conv_ln_g[0], conv_ln_b[0], conv_w_pw2[0], conv_b_pw2[0])
    p2 = p.reshape(-1, PLE_DIM)
    h = _ffn_ple(h.reshape(n, d), p2, 0, ffn_norm_g[0], ffn_w_gate[0], ffn_w_up[0],
                 ffn_w_down[0], ple_norm_g[0], ple_w_gate[0], ple_w_proj[0])

    lambda_init = 0.8 - 0.6 * math.exp(-0.3 * 1)
    q, k, v = _qkv_proj(h, attn_norm_g[0], attn_w_qkv[0], attn_q_norm_g[0], attn_k_norm_g[0])
    d0, d1, lam = _attn_prep(rel_bias, attn_lambda_q1[0], attn_lambda_k1[0], attn_lambda_q2[0],
                             attn_lambda_k2[0], t=t_attn, lambda_init=lambda_init)
    on = _attention(q.reshape(b, s, d), k.reshape(b, s, d), v.reshape(b, s, d), d0, d1, lam,
                    attn_sub_norm_g[0], t=t_attn, out_scale=1.0 - lambda_init)
    h = _ffn_ple(h, p2, 1, ffn_norm_g[1], ffn_w_gate[1], ffn_w_up[1], ffn_w_down[1],
                 ple_norm_g[1], ple_w_gate[1], ple_w_proj[1], pre=(on.reshape(n, d), attn_w_o[0]))
    return h.reshape(b, s, d)
```

```python
import functools
import math

import jax
import jax.numpy as jnp
from jax import lax
from jax.experimental import pallas as pl
from jax.experimental.pallas import tpu as pltpu

D_MODEL = 1024
CONV_WIDTH = 31
N_HEADS = 8
HEAD_DIM = 64
V_DIM = 128
NUM_BUCKETS = 32
MAX_EXACT = 16
MAX_DISTANCE = 128
PLE_DIM = 256
EPS = 1e-6

LANES = 128
MXU_DIM = 256
VMEM_BYTES = 64 * 2 ** 20
HALO = 32
NEG_BIG = -1e30
LOG2E = math.log2(math.e)

F32 = jnp.float32
BF16 = jnp.bfloat16


def _rms(x, g):
    return x * lax.rsqrt(jnp.mean(x * x, axis=-1, keepdims=True) + EPS) * g


def _dot(a, b):
    return jnp.dot(a, b, preferred_element_type=F32)


def _const_spec(shape):
    nd = len(shape)
    return pl.BlockSpec(shape, lambda *_: (0,) * nd, pipeline_mode=pl.Buffered(1))


def _params(semantics, vmem_mib):
    return pltpu.CompilerParams(dimension_semantics=semantics,
                                vmem_limit_bytes=min(vmem_mib * 2 ** 20, VMEM_BYTES - 6 * 2 ** 20))


def _conv_mixer_kernel(x_ref, ng_ref, w1_ref, b1_ref, dw_ref, dwb_ref, lng_ref, lnb_ref,
                       w2_ref, b2_ref, o_ref, ubuf_ref, ybuf_ref, *, tm, rc):
    d = D_MODEL
    nc = d // LANES

    @pl.when(pl.program_id(1) == 0)
    def _():
        ubuf_ref[:, 0:HALO, :] = jnp.zeros((nc, HALO, LANES), F32)

    x = x_ref[0]
    xn = _rms(x, ng_ref[...]).astype(BF16)
    a = _dot(xn, w1_ref[...]) + b1_ref[...]
    u = a[:, :d] * jax.nn.sigmoid(a[:, d:])
    for c in range(nc):
        ubuf_ref[c, HALO:HALO + tm, :] = u[:, c * LANES:(c + 1) * LANES]

    def col_body(c, carry):
        for r0 in range(0, tm, rc):
            acc = None
            for k in range(CONV_WIDTH):
                seg = ubuf_ref[c, pl.ds(r0 + k + HALO - (CONV_WIDTH - 1), rc), :]
                t = seg * dw_ref[c, k:k + 1, :]
                acc = t if acc is None else acc + t
            ybuf_ref[c, r0:r0 + rc, :] = acc
        ubuf_ref[c, 0:HALO, :] = ubuf_ref[c, tm:tm + HALO, :]
        return carry

    lax.fori_loop(0, nc, col_body, 0)

    y = jnp.concatenate([ybuf_ref[c] for c in range(nc)], axis=-1) + dwb_ref[...]
    mu = jnp.mean(y, axis=-1, keepdims=True)
    yc = y - mu
    var = jnp.mean(yc * yc, axis=-1, keepdims=True)
    yn = yc * lax.rsqrt(var + EPS) * lng_ref[...] + lnb_ref[...]
    z = (yn * jax.nn.sigmoid(yn)).astype(BF16)
    o_ref[0] = x + _dot(z, w2_ref[...]) + b2_ref[...]


def _conv_mixer(x, ng, w1, b1, dw, dwb, lng, lnb, w2, b2, *, tm=256, rc=64):
    b, s, d = x.shape
    nc = d // LANES
    dw_p = jnp.pad(dw, ((0, HALO - CONV_WIDTH), (0, 0))).reshape(HALO, nc, LANES).transpose(1, 0, 2)
    row = lambda v: v.reshape(1, -1).astype(F32)
    kern = functools.partial(_conv_mixer_kernel, tm=tm, rc=rc)
    return pl.pallas_call(
        kern,
        out_shape=jax.ShapeDtypeStruct((b, s, d), F32),
        grid=(b, s // tm),
        in_specs=[
            pl.BlockSpec((1, tm, d), lambda i, j: (i, j, 0)),
            _const_spec((1, d)), _const_spec((d, 2 * d)), _const_spec((1, 2 * d)),
            _const_spec((nc, HALO, LANES)), _const_spec((1, d)), _const_spec((1, d)),
            _const_spec((1, d)), _const_spec((d, d)), _const_spec((1, d)),
        ],
        out_specs=pl.BlockSpec((1, tm, d), lambda i, j: (i, j, 0)),
        scratch_shapes=[pltpu.VMEM((nc, HALO + tm, LANES), F32), pltpu.VMEM((nc, tm, LANES), F32)],
        compiler_params=_params(("arbitrary", "arbitrary"), 40),
        name="conv_mixer",
    )(x, row(ng), w1.astype(BF16), row(b1), dw_p, row(dwb), row(lng), row(lnb), w2.astype(BF16), row(b2))


def _ffn_chunks(d_ff):
    step = 2 * MXU_DIM
    return [(c0, min(c0 + step, d_ff)) for c0 in range(0, d_ff, step)]


def _ffn_ple_kernel(*refs, has_pre):
    if has_pre:
        (h_ref, on_ref, wo_ref, fg_ref, wg_ref, wu_ref, wd_ref, p_ref, pg_ref, wpg_ref, wpp_ref,
         o_ref) = refs
        h = h_ref[...] + _dot(on_ref[...], wo_ref[...])
    else:
        (h_ref, fg_ref, wg_ref, wu_ref, wd_ref, p_ref, pg_ref, wpg_ref, wpp_ref, o_ref) = refs
        h = h_ref[...]
    xn = _rms(h, fg_ref[...]).astype(BF16)
    acc = None
    for c0, c1 in _ffn_chunks(wg_ref.shape[1]):
        g = _dot(xn, wg_ref[:, c0:c1])
        u = _dot(xn, wu_ref[:, c0:c1])
        a = (g * jax.nn.sigmoid(g) * u).astype(BF16)
        t = _dot(a, wd_ref[c0:c1, :])
        acc = t if acc is None else acc + t
    h2 = h + acc
    gate = jax.nn.sigmoid(_dot(_rms(h2, pg_ref[...]).astype(BF16), wpg_ref[...]))
    proj = _dot(p_ref[...].astype(BF16), wpp_ref[...])
    o_ref[...] = h2 + gate * proj


def _ffn_ple(h, p, layer, fg, wg, wu, wd, pg, wpg, wpp, pre=None, *, tm=512):
    n, d = h.shape
    p_blk = pl.BlockSpec((tm, PLE_DIM), lambda i: (i + layer * (n // tm), 0))
    d_ff = wg.shape[1]
    row = lambda v: v.reshape(1, -1).astype(F32)
    tok = lambda w: pl.BlockSpec((tm, w), lambda i: (i, 0))
    args, specs = [h], [tok(d)]
    if pre is not None:
        on, wo = pre
        args += [on, wo.astype(BF16)]
        specs += [tok(d), _const_spec((d, d))]
    args += [row(fg), wg.astype(BF16), wu.astype(BF16), wd.astype(BF16), p, row(pg),
             wpg.astype(BF16), wpp.astype(BF16)]
    specs += [_const_spec((1, d)), _const_spec((d, d_ff)), _const_spec((d, d_ff)),
              _const_spec((d_ff, d)), p_blk, _const_spec((1, d)), _const_spec((d, d)),
              _const_spec((PLE_DIM, d))]
    return pl.pallas_call(
        functools.partial(_ffn_ple_kernel, has_pre=pre is not None),
        out_shape=jax.ShapeDtypeStruct((n, d), F32),
        grid=(n // tm,),
        in_specs=specs,
        out_specs=tok(d),
        compiler_params=_params(("parallel",), 56),
        name="ffn_ple_pre" if pre is not None else "ffn_ple",
    )(*args)


def _qkv_kernel(h_ref, ng_ref, w_ref, bd_ref, qg_ref, kg_ref, q_ref, k_ref, v_ref):
    d = D_MODEL
    xn = _rms(h_ref[...], ng_ref[...]).astype(BF16)
    qkv = _dot(xn, w_ref[...])

    def comp_norm(t, gain):
        sq = (t * t).astype(BF16)
        ms = jnp.concatenate([_dot(sq[:, c:c + MXU_DIM], bd_ref[...]) for c in range(0, d, MXU_DIM)],
                             axis=-1)
        return t * lax.rsqrt(ms + EPS) * gain

    q_ref[...] = comp_norm(qkv[:, :d], qg_ref[...]).astype(BF16)
    k_ref[...] = comp_norm(qkv[:, d:2 * d], kg_ref[...]).astype(BF16)
    v_ref[...] = qkv[:, 2 * d:].astype(BF16)


def _qkv_proj(h, ng, w, qg, kg, *, tm=512):
    n, d = h.shape
    reps = d // HEAD_DIM
    grp = jnp.arange(MXU_DIM) // HEAD_DIM
    bd = jnp.where(grp[:, None] == grp[None, :], 1.0 / HEAD_DIM, 0.0).astype(BF16)
    qg_t = (jnp.tile(qg.astype(F32), reps) * (HEAD_DIM ** -0.5 * LOG2E)).reshape(1, d)
    kg_t = jnp.tile(kg.astype(F32), reps).reshape(1, d)
    tok = pl.BlockSpec((tm, d), lambda i: (i, 0))
    out = jax.ShapeDtypeStruct((n, d), BF16)
    return pl.pallas_call(
        _qkv_kernel,
        out_shape=(out, out, out),
        grid=(n // tm,),
        in_specs=[tok, _const_spec((1, d)), _const_spec((d, 3 * d)), _const_spec((MXU_DIM, MXU_DIM)),
                  _const_spec((1, d)), _const_spec((1, d))],
        out_specs=(tok, tok, tok),
        compiler_params=_params(("parallel",), 48),
        name="qkv_proj",
    )(h, ng.reshape(1, d).astype(F32), w.astype(BF16), bd, qg_t, kg_t)


def _attn_prep_kernel(rb_ref, lq1_ref, lk1_ref, lq2_ref, lk2_ref, d0_ref, d1_ref, lam_ref,
                      *, t, lambda_init):
    h = pl.program_id(0)
    far = rb_ref[NUM_BUCKETS - 1, h]
    dist = (lax.broadcasted_iota(jnp.int32, (t, t), 0) - lax.broadcasted_iota(jnp.int32, (t, t), 1))

    def bias_of(dd):
        n = jnp.maximum(dd, 0)
        nf = jnp.maximum(n, 1).astype(F32)
        large = MAX_EXACT + (jnp.log(nf / MAX_EXACT) / math.log(MAX_DISTANCE / MAX_EXACT)
                             * (NUM_BUCKETS - MAX_EXACT)).astype(jnp.int32)
        large = jnp.minimum(large, NUM_BUCKETS - 1)
        bucket = jnp.where(n < MAX_EXACT, n, large)
        out = jnp.zeros(dd.shape, F32)
        for bk in range(NUM_BUCKETS):
            out = jnp.where(bucket == bk, rb_ref[bk, h], out)
        return (out - far) * LOG2E

    d0_ref[0] = jnp.where(dist >= 0, bias_of(dist), NEG_BIG)
    d1_ref[0] = bias_of(dist + t)
    lam = (jnp.exp(jnp.sum(lq1_ref[...] * lk1_ref[...])) - jnp.exp(jnp.sum(lq2_ref[...] * lk2_ref[...]))
           + lambda_init)
    lam_ref[...] = jnp.full(lam_ref.shape, lam, F32)


def _attn_prep(rel_bias, lq1, lk1, lq2, lk2, *, t, lambda_init):
    row = lambda v: v.reshape(1, -1).astype(F32)
    vec = _const_spec((1, HEAD_DIM))
    tab = jax.ShapeDtypeStruct((N_HEADS, t, t), F32)
    return pl.pallas_call(
        functools.partial(_attn_prep_kernel, t=t, lambda_init=lambda_init),
        out_shape=(tab, tab, jax.ShapeDtypeStruct((8, LANES), F32)),
        grid=(N_HEADS,),
        in_specs=[pl.BlockSpec(memory_space=pltpu.SMEM), vec, vec, vec, vec],
        out_specs=(pl.BlockSpec((1, t, t), lambda h: (h, 0, 0)), pl.BlockSpec((1, t, t), lambda h: (h, 0, 0)),
                   pl.BlockSpec((8, LANES), lambda h: (0, 0))),
        compiler_params=_params(("arbitrary",), 32),
        name="attn_prep",
    )(rel_bias.astype(F32), row(lq1), row(lk1), row(lq2), row(lk2))


def _attn_kernel(q_ref, k_ref, v_ref, d0_ref, d1_ref, lam_ref, sg_ref, o_ref,
                 qst_ref, sa_ref, sb_ref, rma_ref, rmb_ref, m_ref, acc_ref, *, t, out_scale):
    n = pl.program_id(2) + 1
    r = 2 * t
    reps = t // LANES

    q = q_ref[0]
    lane = lax.broadcasted_iota(jnp.int32, q.shape, 1)
    zero = jnp.zeros_like(q)
    qst_ref[0:t, :] = jnp.where(lane < HEAD_DIM, q, zero)
    qst_ref[t:r, :] = jnp.where(lane >= HEAD_DIM, q, zero)
    m_ref[...] = jnp.full(m_ref.shape, NEG_BIG, F32)
    acc_ref[...] = jnp.zeros(acc_ref.shape, F32)
    ones = jnp.ones((t, LANES), BF16)

    def logits(c, bias_ref, s_ref, rm_ref):
        k = k_ref[0, pl.ds(pl.multiple_of(c * t, t), t), :]
        s = lax.dot_general(qst_ref[...], k, (((1,), (1,)), ((), ())), preferred_element_type=F32)
        if bias_ref is not None:
            bias = bias_ref[0]
            s = s + jnp.concatenate([bias, bias], axis=0)
        s_ref[...] = s
        rm_ref[...] = jnp.broadcast_to(jnp.max(s, axis=-1, keepdims=True), (r, LANES))

    def accumulate(c, s_ref, rm_ref):
        v = v_ref[0, pl.ds(pl.multiple_of(c * t, t), t), :]
        m_old = m_ref[...]
        m_new = jnp.maximum(m_old, rm_ref[...])
        alpha = jnp.exp2(m_old - m_new)
        p = jnp.exp2(s_ref[...] - jnp.concatenate([m_new] * reps, axis=1)).astype(BF16)
        pv = _dot(p, jnp.concatenate([v, ones], axis=1))
        acc_ref[...] = jnp.concatenate([alpha, alpha], axis=1) * acc_ref[...] + pv
        m_ref[...] = m_new

    buf_a = (sa_ref, rma_ref)
    buf_b = (sb_ref, rmb_ref)
    last = n - 1

    logits(last, d0_ref, *buf_a)

    @pl.when(n >= 2)
    def _():
        logits(last - 1, d1_ref, *buf_b)
        accumulate(last, *buf_a)

    n_pairs = jnp.maximum(n - 2, 0) // 2

    def pair_body(i, carry):
        c = last - 2 * i - 2
        logits(c, None, *buf_a)
        accumulate(c + 1, *buf_b)
        logits(c - 1, None, *buf_b)
        accumulate(c, *buf_a)
        return carry

    lax.fori_loop(0, n_pairs, pair_body, 0)
    odd_tail = jnp.logical_and(n >= 2, n - 2 - 2 * n_pairs == 1)

    @pl.when(odd_tail)
    def _():
        logits(0, None, *buf_a)
        accumulate(1, *buf_b)
        accumulate(0, *buf_a)

    @pl.when(jnp.logical_and(n >= 2, jnp.logical_not(odd_tail)))
    def _():
        accumulate(0, *buf_b)

    @pl.when(n == 1)
    def _():
        accumulate(0, *buf_a)

    acc = acc_ref[...]
    o_all = acc[:, 0:V_DIM] / acc[:, V_DIM:2 * V_DIM]
    lam = lam_ref[0:1, :]
    o = o_all[0:t] - lam * o_all[t:r]
    o_ref[0] = (_rms(o, sg_ref[...]) * out_scale).astype(BF16)


def _attention(q, k, v, d0, d1, lam, sub_g, *, t, out_scale):
    b, s, d = q.shape
    qblk = pl.BlockSpec((1, t, V_DIM), lambda bi, h, i: (bi, i, h))
    kvblk = pl.BlockSpec((1, s, V_DIM), lambda bi, h, i: (bi, 0, h))
    tab = pl.BlockSpec((1, t, t), lambda bi, h, i: (h, 0, 0))
    return pl.pallas_call(
        functools.partial(_attn_kernel, t=t, out_scale=out_scale),
        out_shape=jax.ShapeDtypeStruct((b, s, d), BF16),
        grid=(b, N_HEADS, s // t),
        in_specs=[qblk, kvblk, kvblk, tab, tab, _const_spec((8, LANES)), _const_spec((1, V_DIM))],
        out_specs=qblk,
        scratch_shapes=[pltpu.VMEM((2 * t, V_DIM), BF16), pltpu.VMEM((2 * t, t), F32),
                        pltpu.VMEM((2 * t, t), F32), pltpu.VMEM((2 * t, LANES), F32),
                        pltpu.VMEM((2 * t, LANES), F32), pltpu.VMEM((2 * t, LANES), F32),
                        pltpu.VMEM((2 * t, 2 * V_DIM), F32)],
        compiler_params=_params(("parallel", "parallel", "arbitrary"), 48),
        name="diff_attention",
    )(q, k, v, d0, d1, lam, sub_g.reshape(1, V_DIM).astype(F32))


def kernel(x, p, conv_norm_g, conv_w_pw1, conv_b_pw1, conv_dw_w, conv_dw_b, conv_ln_g, conv_ln_b, conv_w_pw2, conv_b_pw2, attn_norm_g, attn_w_qkv, attn_q_norm_g, attn_k_norm_g, attn_lambda_q1, attn_lambda_k1, attn_lambda_q2, attn_lambda_k2, attn_sub_norm_g, attn_w_o, rel_bias, ffn_norm_g, ffn_w_gate, ffn_w_up, ffn_w_down, ple_norm_g, ple_w_gate, ple_w_proj):
    b, s, d = x.shape
    n = b * s
    t_attn = 512

    h = _conv_mixer(x, conv_norm_g[0], conv_w_pw1[0], conv_b_pw1[0], conv_dw_w[0], conv_dw_b[0],
                    conv_ln_g[0], conv_ln_b[0], conv_w_pw2[0], conv_b_pw2[0])
    p2 = p.reshape(-1, PLE_DIM)
    h = _ffn_ple(h.reshape(n, d), p2, 0, ffn_norm_g[0], ffn_w_gate[0], ffn_w_up[0],
                 ffn_w_down[0], ple_norm_g[0], ple_w_gate[0], ple_w_proj[0])

    lambda_init = 0.8 - 0.6 * math.exp(-0.3 * 1)
    q, k, v = _qkv_proj(h, attn_norm_g[0], attn_w_qkv[0], attn_q_norm_g[0], attn_k_norm_g[0])
    d0, d1, lam = _attn_prep(rel_bias, attn_lambda_q1[0], attn_lambda_k1[0], attn_lambda_q2[0],
                             attn_lambda_k2[0], t=t_attn, lambda_init=lambda_init)
    on = _attention(q.reshape(b, s, d), k.reshape(b, s, d), v.reshape(b, s, d), d0, d1, lam,
                    attn_sub_norm_g[0], t=t_attn, out_scale=1.0 - lambda_init)
    h = _ffn_ple(h, p2, 1, ffn_norm_g[1], ffn_w_gate[1], ffn_w_up[1], ffn_w_down[1],
                 ple_norm_g[1], ple_w_gate[1], ple_w_proj[1], pre=(on.reshape(n, d), attn_w_o[0]))
    return h.reshape(b, s, d)
```

```python
import functools
import math

import jax
import jax.numpy as jnp
from jax import lax
from jax.experimental import pallas as pl
from jax.experimental.pallas import tpu as pltpu

D_MODEL = 1024
CONV_WIDTH = 31
N_HEADS = 8
HEAD_DIM = 64
V_DIM = 128
NUM_BUCKETS = 32
MAX_EXACT = 16
MAX_DISTANCE = 128
PLE_DIM = 256
EPS = 1e-6

LANES = 128
MXU_DIM = 256
VMEM_BYTES = 64 * 2 ** 20
HALO = 32
NEG_BIG = -1e30
LOG2E = math.log2(math.e)

F32 = jnp.float32
BF16 = jnp.bfloat16


def _rms(x, g):
    return x * lax.rsqrt(jnp.mean(x * x, axis=-1, keepdims=True) + EPS) * g


def _dot(a, b):
    return jnp.dot(a, b, preferred_element_type=F32)


def _const_spec(shape):
    nd = len(shape)
    return pl.BlockSpec(shape, lambda *_: (0,) * nd, pipeline_mode=pl.Buffered(1))


def _params(semantics, vmem_mib, flags=None):
    return pltpu.CompilerParams(dimension_semantics=semantics, flags=flags,
                                vmem_limit_bytes=min(vmem_mib * 2 ** 20, VMEM_BYTES - 6 * 2 ** 20))


def _conv_mixer_kernel(x_ref, ng_ref, w1_ref, b1_ref, dw_ref, dwb_ref, lng_ref, lnb_ref,
                       w2_ref, b2_ref, o_ref, ubuf_ref, ybuf_ref, *, tm, rc):
    d = D_MODEL
    nc = d // LANES

    @pl.when(pl.program_id(1) == 0)
    def _():
        ubuf_ref[:, 0:HALO, :] = jnp.zeros((nc, HALO, LANES), F32)

    x = x_ref[0]
    xn = _rms(x, ng_ref[...]).astype(BF16)
    a = _dot(xn, w1_ref[...]) + b1_ref[...]
    u = a[:, :d] * jax.nn.sigmoid(a[:, d:])
    for c in range(nc):
        ubuf_ref[c, HALO:HALO + tm, :] = u[:, c * LANES:(c + 1) * LANES]

    def col_body(c, carry):
        for r0 in range(0, tm, rc):
            acc = None
            for k in range(CONV_WIDTH):
                seg = ubuf_ref[c, pl.ds(r0 + k + HALO - (CONV_WIDTH - 1), rc), :]
                t = seg * dw_ref[c, k:k + 1, :]
                acc = t if acc is None else acc + t
            ybuf_ref[c, r0:r0 + rc, :] = acc
        ubuf_ref[c, 0:HALO, :] = ubuf_ref[c, tm:tm + HALO, :]
        return carry

    lax.fori_loop(0, nc, col_body, 0)

    y = jnp.concatenate([ybuf_ref[c] for c in range(nc)], axis=-1) + dwb_ref[...]
    mu = jnp.mean(y, axis=-1, keepdims=True)
    yc = y - mu
    var = jnp.mean(yc * yc, axis=-1, keepdims=True)
    yn = yc * lax.rsqrt(var + EPS) * lng_ref[...] + lnb_ref[...]
    z = (yn * jax.nn.sigmoid(yn)).astype(BF16)
    o_ref[0] = x + _dot(z, w2_ref[...]) + b2_ref[...]


def _conv_mixer(x, ng, w1, b1, dw, dwb, lng, lnb, w2, b2, *, tm=256, rc=64):
    b, s, d = x.shape
    nc = d // LANES
    dw_p = jnp.pad(dw, ((0, HALO - CONV_WIDTH), (0, 0))).reshape(HALO, nc, LANES).transpose(1, 0, 2)
    row = lambda v: v.reshape(1, -1).astype(F32)
    kern = functools.partial(_conv_mixer_kernel, tm=tm, rc=rc)
    return pl.pallas_call(
        kern,
        out_shape=jax.ShapeDtypeStruct((b, s, d), F32),
        grid=(b, s // tm),
        in_specs=[
            pl.BlockSpec((1, tm, d), lambda i, j: (i, j, 0)),
            _const_spec((1, d)), _const_spec((d, 2 * d)), _const_spec((1, 2 * d)),
            _const_spec((nc, HALO, LANES)), _const_spec((1, d)), _const_spec((1, d)),
            _const_spec((1, d)), _const_spec((d, d)), _const_spec((1, d)),
        ],
        out_specs=pl.BlockSpec((1, tm, d), lambda i, j: (i, j, 0)),
        scratch_shapes=[pltpu.VMEM((nc, HALO + tm, LANES), F32), pltpu.VMEM((nc, tm, LANES), F32)],
        compiler_params=_params(("arbitrary", "arbitrary"), 40),
        name="conv_mixer",
    )(x, row(ng), w1.astype(BF16), row(b1), dw_p, row(dwb), row(lng), row(lnb), w2.astype(BF16), row(b2))


def _ffn_chunks(d_ff):
    step = 2 * MXU_DIM
    return [(c0, min(c0 + step, d_ff)) for c0 in range(0, d_ff, step)]


def _ffn_ple_kernel(*refs, has_pre):
    if has_pre:
        (h_ref, on_ref, wo_ref, fg_ref, wg_ref, wu_ref, wd_ref, p_ref, pg_ref, wpg_ref, wpp_ref,
         o_ref) = refs
        h = h_ref[...] + _dot(on_ref[...], wo_ref[...])
    else:
        (h_ref, fg_ref, wg_ref, wu_ref, wd_ref, p_ref, pg_ref, wpg_ref, wpp_ref, o_ref) = refs
        h = h_ref[...]
    xn = _rms(h, fg_ref[...]).astype(BF16)
    acc = None
    for c0, c1 in _ffn_chunks(wg_ref.shape[1]):
        g = _dot(xn, wg_ref[:, c0:c1])
        u = _dot(xn, wu_ref[:, c0:c1])
        a = (g * jax.nn.sigmoid(g) * u).astype(BF16)
        t = _dot(a, wd_ref[c0:c1, :])
        acc = t if acc is None else acc + t
    h2 = h + acc
    gate = jax.nn.sigmoid(_dot(_rms(h2, pg_ref[...]).astype(BF16), wpg_ref[...]))
    proj = _dot(p_ref[...].astype(BF16), wpp_ref[...])
    o_ref[...] = h2 + gate * proj


def _ffn_ple(h, p, layer, fg, wg, wu, wd, pg, wpg, wpp, pre=None, *, tm=512):
    n, d = h.shape
    p_blk = pl.BlockSpec((tm, PLE_DIM), lambda i: (i + layer * (n // tm), 0))
    d_ff = wg.shape[1]
    row = lambda v: v.reshape(1, -1).astype(F32)
    tok = lambda w: pl.BlockSpec((tm, w), lambda i: (i, 0))
    args, specs = [h], [tok(d)]
    if pre is not None:
        on, wo = pre
        args += [on, wo.astype(BF16)]
        specs += [tok(d), _const_spec((d, d))]
    args += [row(fg), wg.astype(BF16), wu.astype(BF16), wd.astype(BF16), p, row(pg),
             wpg.astype(BF16), wpp.astype(BF16)]
    specs += [_const_spec((1, d)), _const_spec((d, d_ff)), _const_spec((d, d_ff)),
              _const_spec((d_ff, d)), p_blk, _const_spec((1, d)), _const_spec((d, d)),
              _const_spec((PLE_DIM, d))]
    return pl.pallas_call(
        functools.partial(_ffn_ple_kernel, has_pre=pre is not None),
        out_shape=jax.ShapeDtypeStruct((n, d), F32),
        grid=(n // tm,),
        in_specs=specs,
        out_specs=tok(d),
        compiler_params=_params(("parallel",), 56),
        name="ffn_ple_pre" if pre is not None else "ffn_ple",
    )(*args)


def _qkv_kernel(h_ref, ng_ref, wqt_ref, wk_ref, wvt_ref, bd_ref, qg_ref, kg_ref, qt_ref, k_ref, vt_ref):
    d = D_MODEL
    tm = h_ref.shape[0]
    xn = _rms(h_ref[...], ng_ref[...]).astype(BF16)
    nt = (((1,), (1,)), ((), ()))

    qt = lax.dot_general(wqt_ref[...], xn, nt, preferred_element_type=F32).reshape(d // HEAD_DIM, HEAD_DIM, tm)
    qt = qt * lax.rsqrt(jnp.mean(qt * qt, axis=1, keepdims=True) + EPS)
    gain = jnp.concatenate([qg_ref[...]] * (tm // LANES), axis=1)
    qt_ref[0] = (qt.reshape(d, tm) * gain).astype(BF16)

    k = _dot(xn, wk_ref[...])
    sq = (k * k).astype(BF16)
    ms = jnp.concatenate([_dot(sq[:, c:c + MXU_DIM], bd_ref[...]) for c in range(0, d, MXU_DIM)], axis=-1)
    k_ref[...] = (k * lax.rsqrt(ms + EPS) * kg_ref[...]).astype(BF16)

    vt_ref[0] = lax.dot_general(wvt_ref[...], xn, nt, preferred_element_type=F32).astype(BF16)


def _qkv_proj(h, ng, w, qg, kg, *, tm):
    n, d = h.shape
    reps = d // HEAD_DIM
    grp = jnp.arange(MXU_DIM) // HEAD_DIM
    bd = jnp.where(grp[:, None] == grp[None, :], 1.0 / HEAD_DIM, 0.0).astype(BF16)
    qg_t = jnp.tile(qg.astype(F32), reps) * (HEAD_DIM ** -0.5 * LOG2E)
    qg_t = jnp.broadcast_to(qg_t[:, None], (d, LANES))
    kg_t = jnp.tile(kg.astype(F32), reps).reshape(1, d)
    wb = w.astype(BF16)
    wqt, wk, wvt = wb[:, :d].T, wb[:, d:2 * d], wb[:, 2 * d:].T
    tok = pl.BlockSpec((tm, d), lambda i: (i, 0))
    tr = pl.BlockSpec((1, d, tm), lambda i: (i, 0, 0))
    tr_shape = jax.ShapeDtypeStruct((n // tm, d, tm), BF16)
    return pl.pallas_call(
        _qkv_kernel,
        out_shape=(tr_shape, jax.ShapeDtypeStruct((n, d), BF16), tr_shape),
        grid=(n // tm,),
        in_specs=[tok, _const_spec((1, d)), _const_spec((d, d)), _const_spec((d, d)), _const_spec((d, d)),
                  _const_spec((MXU_DIM, MXU_DIM)), _const_spec((d, LANES)), _const_spec((1, d))],
        out_specs=(tr, tok, tr),
        compiler_params=_params(("parallel",), 48),
        name="qkv_proj",
    )(h, ng.reshape(1, d).astype(F32), wqt, wk, wvt, bd, qg_t, kg_t)


def _attn_prep_kernel(rb_ref, lq1_ref, lk1_ref, lq2_ref, lk2_ref, tab_ref, lam_ref,
                      *, t, lambda_init):
    h = pl.program_id(0)
    far = rb_ref[NUM_BUCKETS - 1, h]
    dist = (lax.broadcasted_iota(jnp.int32, (t, t), 1) - lax.broadcasted_iota(jnp.int32, (t, t), 0))

    def bias_of(dd):
        n = jnp.maximum(dd, 0)
        nf = jnp.maximum(n, 1).astype(F32)
        large = MAX_EXACT + (jnp.log(nf / MAX_EXACT) / math.log(MAX_DISTANCE / MAX_EXACT)
                             * (NUM_BUCKETS - MAX_EXACT)).astype(jnp.int32)
        large = jnp.minimum(large, NUM_BUCKETS - 1)
        bucket = jnp.where(n < MAX_EXACT, n, large)
        out = jnp.zeros(dd.shape, F32)
        for bk in range(NUM_BUCKETS):
            out = jnp.where(bucket == bk, rb_ref[bk, h], out)
        return (out - far) * LOG2E

    tab_ref[0, 0] = jnp.where(dist >= 0, bias_of(dist), NEG_BIG)
    tab_ref[0, 1] = bias_of(dist + t)
    tab_ref[0, 2] = jnp.zeros((t, t), F32)
    tab_ref[0, 3] = jnp.full((t, t), NEG_BIG, F32)
    lam = (jnp.exp(jnp.sum(lq1_ref[...] * lk1_ref[...])) - jnp.exp(jnp.sum(lq2_ref[...] * lk2_ref[...]))
           + lambda_init)
    lam_ref[...] = jnp.full(lam_ref.shape, lam, F32)


def _attn_prep(rel_bias, lq1, lk1, lq2, lk2, *, t, lambda_init):
    row = lambda v: v.reshape(1, -1).astype(F32)
    vec = _const_spec((1, HEAD_DIM))
    tab = jax.ShapeDtypeStruct((N_HEADS, 4, t, t), F32)
    return pl.pallas_call(
        functools.partial(_attn_prep_kernel, t=t, lambda_init=lambda_init),
        out_shape=(tab, jax.ShapeDtypeStruct((8, LANES), F32)),
        grid=(N_HEADS,),
        in_specs=[pl.BlockSpec(memory_space=pltpu.SMEM), vec, vec, vec, vec],
        out_specs=(pl.BlockSpec((1, 4, t, t), lambda h: (h, 0, 0, 0)), pl.BlockSpec((8, LANES), lambda h: (0, 0))),
        compiler_params=_params(("arbitrary",), 32),
        name="attn_prep",
    )(rel_bias.astype(F32), row(lq1), row(lk1), row(lq2), row(lk2))


ONES_ROWS = 16
SUBLANES = 8


def _attn_kernel(qt_ref, k_ref, vt_ref, tab_ref, lam_ref, sg_ref, o_ref,
                 qst_ref, sa_ref, sb_ref, rma_ref, rmb_ref, m_ref, acc_ref, *, t, nq, out_scale):
    r = 2 * t
    sub = lax.broadcasted_iota(jnp.int32, (V_DIM, t), 0)
    ones = jnp.ones((ONES_ROWS, t), BF16)
    assert nq >= 3

    def build_queries(qi):
        qt = qt_ref[0, qi]
        zero = jnp.zeros_like(qt)
        slot = lax.rem(qi, 2)
        qst_ref[slot, :, 0:t] = jnp.where(sub < HEAD_DIM, qt, zero)
        qst_ref[slot, :, t:r] = jnp.where(sub >= HEAD_DIM, qt, zero)

    col_blocks = [slice(j, j + MXU_DIM) for j in range(0, r, MXU_DIM)]

    def logits_block(qi, c, table, cols, s_ref, rm_ref):
        k = k_ref[0, pl.ds(pl.multiple_of(c * t, t), t), :]
        s = _dot(k, qst_ref[lax.rem(qi, 2), :, cols])
        if table is not None:
            bias_cols = slice(cols.start % t, cols.start % t + MXU_DIM)
            s = s + tab_ref[0, table, :, bias_cols]
        s_ref[:, cols] = s
        rm_ref[:, cols] = jnp.broadcast_to(jnp.max(s, axis=0, keepdims=True), (SUBLANES, MXU_DIM))

    def accumulate_block(qi, c, first, cols, s_ref, rm_ref):
        m_new = rm_ref[:, cols] if first else jnp.maximum(m_ref[qi, :, cols], rm_ref[:, cols])
        p = jnp.exp2(s_ref[:, cols] - jnp.tile(m_new, (t // SUBLANES, 1))).astype(BF16)
        pv = _dot(jnp.concatenate([vt_ref[0, c], ones], axis=0), p)
        if not first:
            alpha = jnp.exp2(m_ref[qi, :, cols] - m_new)
            pv = jnp.tile(alpha, (acc_ref.shape[1] // SUBLANES, 1)) * acc_ref[qi, :, cols] + pv
        acc_ref[qi, :, cols] = pv
        m_ref[qi, :, cols] = m_new

    def step(cur, first, nxt, table, buf_cur, buf_nxt):
        for cols in col_blocks:
            if nxt is not None:
                logits_block(*nxt, table, cols, *buf_nxt)
            accumulate_block(*cur, first, cols, *buf_cur)

    buf_a = (sa_ref, rma_ref)
    buf_b = (sb_ref, rmb_ref)
    build_queries(0)
    for cols in col_blocks:
        logits_block(0, 0, 0, cols, *buf_a)

    def near_trip(qi, carry):
        end = qi == nq - 1
        q_up = jnp.where(end, 2, qi + 1)
        c_prev = jnp.maximum(qi - 1, 0)
        build_queries(q_up)
        step((qi, qi), True, (qi, c_prev), jnp.where(qi == 0, 3, 1), buf_a, buf_b)
        step((qi, c_prev), False, (q_up, jnp.where(end, 0, q_up)), jnp.where(end, 2, 0), buf_b, buf_a)
        return carry

    lax.fori_loop(0, nq, near_trip, 0)

    def following(qi, c):
        wrap = c == 0
        qn = jnp.where(wrap, jnp.minimum(qi + 1, nq - 1), qi)
        return qn, jnp.where(wrap, qn - 2, c - 1)

    def far_step(cur, buf_cur, buf_nxt):
        qi = cur[0]
        build_queries(jnp.where(qi + 1 < nq, qi + 1, qi - 1))
        nxt = following(*cur)
        step(cur, False, nxt, None, buf_cur, buf_nxt)
        return nxt

    def far_trip(i, cur):
        return far_step(far_step(cur, buf_a, buf_b), buf_b, buf_a)

    n_far = (nq - 1) * (nq - 2) // 2
    assert n_far % 2 == 1
    cur = lax.fori_loop(0, n_far // 2, far_trip, (jnp.int32(2), jnp.int32(0)))
    step(cur, False, None, None, buf_a, buf_b)

    lam = jnp.tile(lam_ref[...], (V_DIM // SUBLANES, t // LANES))

    def finish(qi, carry):
        acc = acc_ref[qi]
        o_all = acc[0:V_DIM] / jnp.tile(acc[V_DIM:V_DIM + SUBLANES], (V_DIM // SUBLANES, 1))
        o = (o_all[:, 0:t] - lam * o_all[:, t:r]).T
        o_ref[0, pl.ds(pl.multiple_of(qi * t, t), t), :] = (_rms(o, sg_ref[...]) * out_scale).astype(BF16)
        return carry

    lax.fori_loop(0, nq, finish, 0)


def _attention(qt, k, vt, tab, lam, sub_g, *, t, out_scale):
    b, s, d = k.shape
    nq = s // t
    rowmajor = pl.BlockSpec((1, s, V_DIM), lambda bi, h: (bi, 0, h))
    featmajor = pl.BlockSpec((1, nq, V_DIM, t), lambda bi, h: (bi, 0, h, 0))
    return pl.pallas_call(
        functools.partial(_attn_kernel, t=t, nq=nq, out_scale=out_scale),
        out_shape=jax.ShapeDtypeStruct((b, s, d), BF16),
        grid=(b, N_HEADS),
        in_specs=[featmajor, rowmajor, featmajor, pl.BlockSpec((1, 4, t, t), lambda bi, h: (h, 0, 0, 0)),
                  _const_spec((8, LANES)), _const_spec((1, V_DIM))],
        out_specs=rowmajor,
        scratch_shapes=[pltpu.VMEM((2, V_DIM, 2 * t), BF16),
                        pltpu.VMEM((t, 2 * t), F32), pltpu.VMEM((t, 2 * t), F32),
                        pltpu.VMEM((SUBLANES, 2 * t), F32), pltpu.VMEM((SUBLANES, 2 * t), F32),
                        pltpu.VMEM((nq, SUBLANES, 2 * t), F32),
                        pltpu.VMEM((nq, V_DIM + ONES_ROWS, 2 * t), F32)],
        compiler_params=_params(("parallel", "parallel"), 58),
        name="diff_attention",
    )(qt, k, vt, tab, lam, sub_g.reshape(1, V_DIM).astype(F32))


def kernel(x, p, conv_norm_g, conv_w_pw1, conv_b_pw1, conv_dw_w, conv_dw_b, conv_ln_g, conv_ln_b, conv_w_pw2, conv_b_pw2, attn_norm_g, attn_w_qkv, attn_q_norm_g, attn_k_norm_g, attn_lambda_q1, attn_lambda_k1, attn_lambda_q2, attn_lambda_k2, attn_sub_norm_g, attn_w_o, rel_bias, ffn_norm_g, ffn_w_gate, ffn_w_up, ffn_w_down, ple_norm_g, ple_w_gate, ple_w_proj):
    b, s, d = x.shape
    n = b * s
    t_attn = 512

    h = _conv_mixer(x, conv_norm_g[0], conv_w_pw1[0], conv_b_pw1[0], conv_dw_w[0], conv_dw_b[0],
                    conv_ln_g[0], conv_ln_b[0], conv_w_pw2[0], conv_b_pw2[0])
    p2 = p.reshape(-1, PLE_DIM)
    h = _ffn_ple(h.reshape(n, d), p2, 0, ffn_norm_g[0], ffn_w_gate[0], ffn_w_up[0],
                 ffn_w_down[0], ple_norm_g[0], ple_w_gate[0], ple_w_proj[0])

    lambda_init = 0.8 - 0.6 * math.exp(-0.3 * 1)
    qt, k, vt = _qkv_proj(h, attn_norm_g[0], attn_w_qkv[0], attn_q_norm_g[0], attn_k_norm_g[0], tm=t_attn)
    nq = s // t_attn
    tab, lam = _attn_prep(rel_bias, attn_lambda_q1[0], attn_lambda_k1[0], attn_lambda_q2[0],
                          attn_lambda_k2[0], t=t_attn, lambda_init=lambda_init)
    on = _attention(qt.reshape(b, nq, d, t_attn), k.reshape(b, s, d), vt.reshape(b, nq, d, t_attn), tab, lam,
                    attn_sub_norm_g[0], t=t_attn, out_scale=1.0 - lambda_init)
    h = _ffn_ple(h, p2, 1, ffn_norm_g[1], ffn_w_gate[1], ffn_w_up[1], ffn_w_down[1],
                 ple_norm_g[1], ple_w_gate[1], ple_w_proj[1], pre=(on.reshape(n, d), attn_w_o[0]))
    return h.reshape(b, s, d)
```

```python
import functools
import math

import jax
import jax.numpy as jnp
from jax import lax
from jax.experimental import pallas as pl
from jax.experimental.pallas import tpu as pltpu

D_MODEL = 1024
CONV_WIDTH = 31
N_HEADS = 8
HEAD_DIM = 64
V_DIM = 128
NUM_BUCKETS = 32
MAX_EXACT = 16
MAX_DISTANCE = 128
PLE_DIM = 256
EPS = 1e-6

LANES = 128
MXU_DIM = 256
VMEM_BYTES = 64 * 2 ** 20
HALO = 32
NEG_BIG = -1e30
LOG2E = math.log2(math.e)

F32 = jnp.float32
BF16 = jnp.bfloat16


def _rms(x, g):
    return x * lax.rsqrt(jnp.mean(x * x, axis=-1, keepdims=True) + EPS) * g


def _dot(a, b):
    return jnp.dot(a, b, preferred_element_type=F32)


def _const_spec(shape):
    nd = len(shape)
    return pl.BlockSpec(shape, lambda *_: (0,) * nd, pipeline_mode=pl.Buffered(1))


def _params(semantics, vmem_mib, flags=None):
    return pltpu.CompilerParams(dimension_semantics=semantics, flags=flags,
                                vmem_limit_bytes=min(vmem_mib * 2 ** 20, VMEM_BYTES - 6 * 2 ** 20))


def _conv_mixer_kernel(x_ref, ng_ref, w1_ref, b1_ref, dw_ref, dwb_ref, lng_ref, lnb_ref,
                       w2_ref, b2_ref, o_ref, ubuf_ref, ybuf_ref, *, tm, rc):
    d = D_MODEL
    nc = d // LANES

    @pl.when(pl.program_id(1) == 0)
    def _():
        ubuf_ref[:, 0:HALO, :] = jnp.zeros((nc, HALO, LANES), F32)

    x = x_ref[0]
    xn = _rms(x, ng_ref[...]).astype(BF16)

    def conv_chunk(c):
        for r0 in range(0, tm, rc):
            acc = None
            for k in range(CONV_WIDTH):
                seg = ubuf_ref[c, pl.ds(r0 + k + HALO - (CONV_WIDTH - 1), rc), :]
                t = seg * dw_ref[c, k:k + 1, :]
                acc = t if acc is None else acc + t
            ybuf_ref[c, r0:r0 + rc, :] = acc
        ubuf_ref[c, 0:HALO, :] = ubuf_ref[c, tm:tm + HALO, :]

    for c0 in range(0, d, MXU_DIM):
        val = _dot(xn, w1_ref[:, c0:c0 + MXU_DIM]) + b1_ref[:, c0:c0 + MXU_DIM]
        gate = _dot(xn, w1_ref[:, d + c0:d + c0 + MXU_DIM]) + b1_ref[:, d + c0:d + c0 + MXU_DIM]
        u = val * jax.nn.sigmoid(gate)
        for h in range(MXU_DIM // LANES):
            c = c0 // LANES + h
            ubuf_ref[c, HALO:HALO + tm, :] = u[:, h * LANES:(h + 1) * LANES]
            conv_chunk(c)

    y = jnp.concatenate([ybuf_ref[c] for c in range(nc)], axis=-1) + dwb_ref[...]
    mu = jnp.mean(y, axis=-1, keepdims=True)
    yc = y - mu
    var = jnp.mean(yc * yc, axis=-1, keepdims=True)
    yn = yc * lax.rsqrt(var + EPS) * lng_ref[...] + lnb_ref[...]
    z = (yn * jax.nn.sigmoid(yn)).astype(BF16)
    o_ref[0] = x + _dot(z, w2_ref[...]) + b2_ref[...]


def _conv_mixer(x, ng, w1, b1, dw, dwb, lng, lnb, w2, b2, *, tm=256, rc=64):
    b, s, d = x.shape
    nc = d // LANES
    dw_p = jnp.pad(dw, ((0, HALO - CONV_WIDTH), (0, 0))).reshape(HALO, nc, LANES).transpose(1, 0, 2)
    row = lambda v: v.reshape(1, -1).astype(F32)
    kern = functools.partial(_conv_mixer_kernel, tm=tm, rc=rc)
    return pl.pallas_call(
        kern,
        out_shape=jax.ShapeDtypeStruct((b, s, d), F32),
        grid=(b, s // tm),
        in_specs=[
            pl.BlockSpec((1, tm, d), lambda i, j: (i, j, 0)),
            _const_spec((1, d)), _const_spec((d, 2 * d)), _const_spec((1, 2 * d)),
            _const_spec((nc, HALO, LANES)), _const_spec((1, d)), _const_spec((1, d)),
            _const_spec((1, d)), _const_spec((d, d)), _const_spec((1, d)),
        ],
        out_specs=pl.BlockSpec((1, tm, d), lambda i, j: (i, j, 0)),
        scratch_shapes=[pltpu.VMEM((nc, HALO + tm, LANES), F32), pltpu.VMEM((nc, tm, LANES), F32)],
        compiler_params=_params(("arbitrary", "arbitrary"), 40),
        name="conv_mixer",
    )(x, row(ng), w1.astype(BF16), row(b1), dw_p, row(dwb), row(lng), row(lnb), w2.astype(BF16), row(b2))


def _ffn_chunks(d_ff):
    step = 2 * MXU_DIM
    return [(c0, min(c0 + step, d_ff)) for c0 in range(0, d_ff, step)]


def _ffn_ple_kernel(*refs, has_pre):
    if has_pre:
        (h_ref, on_ref, wo_ref, fg_ref, wg_ref, wu_ref, wd_ref, p_ref, pg_ref, wpg_ref, wpp_ref,
         o_ref) = refs
        h = h_ref[...] + _dot(on_ref[...], wo_ref[...])
    else:
        (h_ref, fg_ref, wg_ref, wu_ref, wd_ref, p_ref, pg_ref, wpg_ref, wpp_ref, o_ref) = refs
        h = h_ref[...]
    xn = _rms(h, fg_ref[...]).astype(BF16)
    acc = None
    for c0, c1 in _ffn_chunks(wg_ref.shape[1]):
        g = _dot(xn, wg_ref[:, c0:c1])
        u = _dot(xn, wu_ref[:, c0:c1])
        a = (g * jax.nn.sigmoid(g) * u).astype(BF16)
        t = _dot(a, wd_ref[c0:c1, :])
        acc = t if acc is None else acc + t
    h2 = h + acc
    gate = jax.nn.sigmoid(_dot(_rms(h2, pg_ref[...]).astype(BF16), wpg_ref[...]))
    proj = _dot(p_ref[...].astype(BF16), wpp_ref[...])
    o_ref[...] = h2 + gate * proj


def _ffn_ple(h, p, layer, fg, wg, wu, wd, pg, wpg, wpp, pre=None, *, tm=512):
    n, d = h.shape
    p_blk = pl.BlockSpec((tm, PLE_DIM), lambda i: (i + layer * (n // tm), 0))
    d_ff = wg.shape[1]
    row = lambda v: v.reshape(1, -1).astype(F32)
    tok = lambda w: pl.BlockSpec((tm, w), lambda i: (i, 0))
    args, specs = [h], [tok(d)]
    if pre is not None:
        on, wo = pre
        args += [on, wo.astype(BF16)]
        specs += [tok(d), _const_spec((d, d))]
    args += [row(fg), wg.astype(BF16), wu.astype(BF16), wd.astype(BF16), p, row(pg),
             wpg.astype(BF16), wpp.astype(BF16)]
    specs += [_const_spec((1, d)), _const_spec((d, d_ff)), _const_spec((d, d_ff)),
              _const_spec((d_ff, d)), p_blk, _const_spec((1, d)), _const_spec((d, d)),
              _const_spec((PLE_DIM, d))]
    return pl.pallas_call(
        functools.partial(_ffn_ple_kernel, has_pre=pre is not None),
        out_shape=jax.ShapeDtypeStruct((n, d), F32),
        grid=(n // tm,),
        in_specs=specs,
        out_specs=tok(d),
        compiler_params=_params(("parallel",), 56),
        name="ffn_ple_pre" if pre is not None else "ffn_ple",
    )(*args)


def _qkv_kernel(h_ref, ng_ref, wqt_ref, wk_ref, wvt_ref, bd_ref, qg_ref, kg_ref, qt_ref, k_ref, vt_ref):
    d = D_MODEL
    tm = h_ref.shape[0]
    xn = _rms(h_ref[...], ng_ref[...]).astype(BF16)
    nt = (((1,), (1,)), ((), ()))

    qt = lax.dot_general(wqt_ref[...], xn, nt, preferred_element_type=F32).reshape(d // HEAD_DIM, HEAD_DIM, tm)
    qt = qt * lax.rsqrt(jnp.mean(qt * qt, axis=1, keepdims=True) + EPS)
    gain = jnp.concatenate([qg_ref[...]] * (tm // LANES), axis=1)
    qt_ref[0] = (qt.reshape(d, tm) * gain).astype(BF16)

    k = _dot(xn, wk_ref[...])
    sq = (k * k).astype(BF16)
    ms = jnp.concatenate([_dot(sq[:, c:c + MXU_DIM], bd_ref[...]) for c in range(0, d, MXU_DIM)], axis=-1)
    k_ref[...] = (k * lax.rsqrt(ms + EPS) * kg_ref[...]).astype(BF16)

    vt_ref[0] = lax.dot_general(wvt_ref[...], xn, nt, preferred_element_type=F32).astype(BF16)


def _qkv_proj(h, ng, w, qg, kg, *, tm):
    n, d = h.shape
    reps = d // HEAD_DIM
    grp = jnp.arange(MXU_DIM) // HEAD_DIM
    bd = jnp.where(grp[:, None] == grp[None, :], 1.0 / HEAD_DIM, 0.0).astype(BF16)
    qg_t = jnp.tile(qg.astype(F32), reps) * (HEAD_DIM ** -0.5 * LOG2E)
    qg_t = jnp.broadcast_to(qg_t[:, None], (d, LANES))
    kg_t = jnp.tile(kg.astype(F32), reps).reshape(1, d)
    wb = w.astype(BF16)
    wqt, wk, wvt = wb[:, :d].T, wb[:, d:2 * d], wb[:, 2 * d:].T
    tok = pl.BlockSpec((tm, d), lambda i: (i, 0))
    tr = pl.BlockSpec((1, d, tm), lambda i: (i, 0, 0))
    tr_shape = jax.ShapeDtypeStruct((n // tm, d, tm), BF16)
    return pl.pallas_call(
        _qkv_kernel,
        out_shape=(tr_shape, jax.ShapeDtypeStruct((n, d), BF16), tr_shape),
        grid=(n // tm,),
        in_specs=[tok, _const_spec((1, d)), _const_spec((d, d)), _const_spec((d, d)), _const_spec((d, d)),
                  _const_spec((MXU_DIM, MXU_DIM)), _const_spec((d, LANES)), _const_spec((1, d))],
        out_specs=(tr, tok, tr),
        compiler_params=_params(("parallel",), 48),
        name="qkv_proj",
    )(h, ng.reshape(1, d).astype(F32), wqt, wk, wvt, bd, qg_t, kg_t)


def _attn_prep_kernel(rb_ref, lq1_ref, lk1_ref, lq2_ref, lk2_ref, tab_ref, lam_ref,
                      *, t, lambda_init):
    h = pl.program_id(0)
    far = rb_ref[NUM_BUCKETS - 1, h]
    dist = (lax.broadcasted_iota(jnp.int32, (t, t), 1) - lax.broadcasted_iota(jnp.int32, (t, t), 0))

    def bias_of(dd):
        n = jnp.maximum(dd, 0)
        nf = jnp.maximum(n, 1).astype(F32)
        large = MAX_EXACT + (jnp.log(nf / MAX_EXACT) / math.log(MAX_DISTANCE / MAX_EXACT)
                             * (NUM_BUCKETS - MAX_EXACT)).astype(jnp.int32)
        large = jnp.minimum(large, NUM_BUCKETS - 1)
        bucket = jnp.where(n < MAX_EXACT, n, large)
        out = jnp.zeros(dd.shape, F32)
        for bk in range(NUM_BUCKETS):
            out = jnp.where(bucket == bk, rb_ref[bk, h], out)
        return (out - far) * LOG2E

    tab_ref[0, 0] = jnp.where(dist >= 0, bias_of(dist), NEG_BIG)
    tab_ref[0, 1] = bias_of(dist + t)
    tab_ref[0, 2] = jnp.zeros((t, t), F32)
    tab_ref[0, 3] = jnp.full((t, t), NEG_BIG, F32)
    lam = (jnp.exp(jnp.sum(lq1_ref[...] * lk1_ref[...])) - jnp.exp(jnp.sum(lq2_ref[...] * lk2_ref[...]))
           + lambda_init)
    lam_ref[...] = jnp.full(lam_ref.shape, lam, F32)


def _attn_prep(rel_bias, lq1, lk1, lq2, lk2, *, t, lambda_init):
    row = lambda v: v.reshape(1, -1).astype(F32)
    vec = _const_spec((1, HEAD_DIM))
    tab = jax.ShapeDtypeStruct((N_HEADS, 4, t, t), F32)
    return pl.pallas_call(
        functools.partial(_attn_prep_kernel, t=t, lambda_init=lambda_init),
        out_shape=(tab, jax.ShapeDtypeStruct((8, LANES), F32)),
        grid=(N_HEADS,),
        in_specs=[pl.BlockSpec(memory_space=pltpu.SMEM), vec, vec, vec, vec],
        out_specs=(pl.BlockSpec((1, 4, t, t), lambda h: (h, 0, 0, 0)), pl.BlockSpec((8, LANES), lambda h: (0, 0))),
        compiler_params=_params(("arbitrary",), 32),
        name="attn_prep",
    )(rel_bias.astype(F32), row(lq1), row(lk1), row(lq2), row(lk2))


ONES_ROWS = 16
SUBLANES = 8


def _attn_kernel(qt_ref, k_ref, vt_ref, tab_ref, lam_ref, sg_ref, o_ref,
                 qst_ref, sa_ref, sb_ref, rma_ref, rmb_ref, m_ref, acc_ref, *, t, nq, out_scale):
    r = 2 * t
    sub = lax.broadcasted_iota(jnp.int32, (V_DIM, t), 0)
    ones = jnp.ones((ONES_ROWS, t), BF16)
    assert nq >= 3

    def build_queries(qi):
        qt = qt_ref[0, qi]
        zero = jnp.zeros_like(qt)
        slot = lax.rem(qi, 2)
        qst_ref[slot, :, 0:t] = jnp.where(sub < HEAD_DIM, qt, zero)
        qst_ref[slot, :, t:r] = jnp.where(sub >= HEAD_DIM, qt, zero)

    col_blocks = [slice(j, j + MXU_DIM) for j in range(0, r, MXU_DIM)]

    def logits_block(qi, c, table, cols, s_ref, rm_ref):
        k = k_ref[0, pl.ds(pl.multiple_of(c * t, t), t), :]
        s = _dot(k, qst_ref[lax.rem(qi, 2), :, cols])
        if table is not None:
            bias_cols = slice(cols.start % t, cols.start % t + MXU_DIM)
            s = s + tab_ref[0, table, :, bias_cols]
        s_ref[:, cols] = s
        rm_ref[:, cols] = jnp.broadcast_to(jnp.max(s, axis=0, keepdims=True), (SUBLANES, MXU_DIM))

    def accumulate_block(qi, c, first, cols, s_ref, rm_ref):
        m_new = rm_ref[:, cols] if first else jnp.maximum(m_ref[qi, :, cols], rm_ref[:, cols])
        p = jnp.exp2(s_ref[:, cols] - jnp.tile(m_new, (t // SUBLANES, 1))).astype(BF16)
        pv = _dot(jnp.concatenate([vt_ref[0, c], ones], axis=0), p)
        if not first:
            alpha = jnp.exp2(m_ref[qi, :, cols] - m_new)
            pv = jnp.tile(alpha, (acc_ref.shape[1] // SUBLANES, 1)) * acc_ref[qi, :, cols] + pv
        acc_ref[qi, :, cols] = pv
        m_ref[qi, :, cols] = m_new

    def step(cur, first, nxt, table, buf_cur, buf_nxt):
        for cols in col_blocks:
            if nxt is not None:
                logits_block(*nxt, table, cols, *buf_nxt)
            accumulate_block(*cur, first, cols, *buf_cur)

    buf_a = (sa_ref, rma_ref)
    buf_b = (sb_ref, rmb_ref)
    build_queries(0)
    for cols in col_blocks:
        logits_block(0, 0, 0, cols, *buf_a)

    def near_block(qi):
        end = qi == nq - 1
        q_up = jnp.where(end, 2, qi + 1)
        c_prev = jnp.maximum(qi - 1, 0)
        build_queries(q_up)
        step((qi, qi), True, (qi, c_prev), jnp.where(qi == 0, 3, 1), buf_a, buf_b)
        step((qi, c_prev), False, (q_up, jnp.where(end, 0, q_up)), jnp.where(end, 2, 0), buf_b, buf_a)

    near_unroll = 2
    assert nq % near_unroll == 0

    def near_trip(i, carry):
        for j in range(near_unroll):
            near_block(near_unroll * i + j)
        return carry

    lax.fori_loop(0, nq // near_unroll, near_trip, 0)

    def following(qi, c):
        wrap = c == 0
        qn = jnp.where(wrap, jnp.minimum(qi + 1, nq - 1), qi)
        return qn, jnp.where(wrap, qn - 2, c - 1)

    def far_step(cur, buf_cur, buf_nxt):
        qi = cur[0]
        build_queries(jnp.where(qi + 1 < nq, qi + 1, qi - 1))
        nxt = following(*cur)
        step(cur, False, nxt, None, buf_cur, buf_nxt)
        return nxt

    far_unroll = 8

    def far_trip(i, cur):
        for _ in range(far_unroll // 2):
            cur = far_step(far_step(cur, buf_a, buf_b), buf_b, buf_a)
        return cur

    n_far = (nq - 1) * (nq - 2) // 2
    assert n_far % far_unroll == 1
    cur = lax.fori_loop(0, n_far // far_unroll, far_trip, (jnp.int32(2), jnp.int32(0)))
    step(cur, False, None, None, buf_a, buf_b)

    lam = jnp.tile(lam_ref[...], (V_DIM // SUBLANES, t // LANES))

    def finish(qi, carry):
        acc = acc_ref[qi]
        o_all = acc[0:V_DIM] / jnp.tile(acc[V_DIM:V_DIM + SUBLANES], (V_DIM // SUBLANES, 1))
        o = (o_all[:, 0:t] - lam * o_all[:, t:r]).T
        o_ref[0, pl.ds(pl.multiple_of(qi * t, t), t), :] = (_rms(o, sg_ref[...]) * out_scale).astype(BF16)
        return carry

    lax.fori_loop(0, nq, finish, 0)


def _attention(qt, k, vt, tab, lam, sub_g, *, t, out_scale):
    b, s, d = k.shape
    nq = s // t
    rowmajor = pl.BlockSpec((1, s, V_DIM), lambda bi, h: (bi, 0, h))
    featmajor = pl.BlockSpec((1, nq, V_DIM, t), lambda bi, h: (bi, 0, h, 0))
    return pl.pallas_call(
        functools.partial(_attn_kernel, t=t, nq=nq, out_scale=out_scale),
        out_shape=jax.ShapeDtypeStruct((b, s, d), BF16),
        grid=(b, N_HEADS),
        in_specs=[featmajor, rowmajor, featmajor, pl.BlockSpec((1, 4, t, t), lambda bi, h: (h, 0, 0, 0)),
                  _const_spec((8, LANES)), _const_spec((1, V_DIM))],
        out_specs=rowmajor,
        scratch_shapes=[pltpu.VMEM((2, V_DIM, 2 * t), BF16),
                        pltpu.VMEM((t, 2 * t), F32), pltpu.VMEM((t, 2 * t), F32),
                        pltpu.VMEM((SUBLANES, 2 * t), F32), pltpu.VMEM((SUBLANES, 2 * t), F32),
                        pltpu.VMEM((nq, SUBLANES, 2 * t), F32),
                        pltpu.VMEM((nq, V_DIM + ONES_ROWS, 2 * t), F32)],
        compiler_params=_params(("parallel", "parallel"), 58),
        name="diff_attention",
    )(qt, k, vt, tab, lam, sub_g.reshape(1, V_DIM).astype(F32))


def kernel(x, p, conv_norm_g, conv_w_pw1, conv_b_pw1, conv_dw_w, conv_dw_b, conv_ln_g, conv_ln_b, conv_w_pw2, conv_b_pw2, attn_norm_g, attn_w_qkv, attn_q_norm_g, attn_k_norm_g, attn_lambda_q1, attn_lambda_k1, attn_lambda_q2, attn_lambda_k2, attn_sub_norm_g, attn_w_o, rel_bias, ffn_norm_g, ffn_w_gate, ffn_w_up, ffn_w_down, ple_norm_g, ple_w_gate, ple_w_proj):
    b, s, d = x.shape
    n = b * s
    t_attn = 512

    h = _conv_mixer(x, conv_norm_g[0], conv_w_pw1[0], conv_b_pw1[0], conv_dw_w[0], conv_dw_b[0],
                    conv_ln_g[0], conv_ln_b[0], conv_w_pw2[0], conv_b_pw2[0])
    p2 = p.reshape(-1, PLE_DIM)
    h = _ffn_ple(h.reshape(n, d), p2, 0, ffn_norm_g[0], ffn_w_gate[0], ffn_w_up[0],
                 ffn_w_down[0], ple_norm_g[0], ple_w_gate[0], ple_w_proj[0])

    lambda_init = 0.8 - 0.6 * math.exp(-0.3 * 1)
    qt, k, vt = _qkv_proj(h, attn_norm_g[0], attn_w_qkv[0], attn_q_norm_g[0], attn_k_norm_g[0], tm=t_attn)
    nq = s // t_attn
    tab, lam = _attn_prep(rel_bias, attn_lambda_q1[0], attn_lambda_k1[0], attn_lambda_q2[0],
                          attn_lambda_k2[0], t=t_attn, lambda_init=lambda_init)
    on = _attention(qt.reshape(b, nq, d, t_attn), k.reshape(b, s, d), vt.reshape(b, nq, d, t_attn), tab, lam,
                    attn_sub_norm_g[0], t=t_attn, out_scale=1.0 - lambda_init)
    h = _ffn_ple(h, p2, 1, ffn_norm_g[1], ffn_w_gate[1], ffn_w_up[1], ffn_w_down[1],
                 ple_norm_g[1], ple_w_gate[1], ple_w_proj[1], pre=(on.reshape(n, d), attn_w_o[0]))
    return h.reshape(b, s, d)
```

```python
import functools
import math

import jax
import jax.numpy as jnp
from jax import lax
from jax.experimental import pallas as pl
from jax.experimental.pallas import tpu as pltpu

D_MODEL = 1024
CONV_WIDTH = 31
N_HEADS = 8
HEAD_DIM = 64
V_DIM = 128
NUM_BUCKETS = 32
MAX_EXACT = 16
MAX_DISTANCE = 128
PLE_DIM = 256
EPS = 1e-6

LANES = 128
MXU_DIM = 256
VMEM_BYTES = 64 * 2 ** 20
HALO = 32
NEG_BIG = -1e30
LOG2E = math.log2(math.e)

F32 = jnp.float32
BF16 = jnp.bfloat16


def _rms(x, g):
    return x * lax.rsqrt(jnp.mean(x * x, axis=-1, keepdims=True) + EPS) * g


def _dot(a, b):
    return jnp.dot(a, b, preferred_element_type=F32)


def _const_spec(shape):
    nd = len(shape)
    return pl.BlockSpec(shape, lambda *_: (0,) * nd, pipeline_mode=pl.Buffered(1))


def _params(semantics, vmem_mib, flags=None):
    return pltpu.CompilerParams(dimension_semantics=semantics, flags=flags,
                                vmem_limit_bytes=min(vmem_mib * 2 ** 20, VMEM_BYTES - 6 * 2 ** 20))


def _conv_mixer_kernel(x_ref, ng_ref, w1_ref, b1_ref, dw_ref, dwb_ref, lng_ref, lnb_ref,
                       w2_ref, b2_ref, o_ref, ubuf_ref, ybuf_ref, *, tm, rc):
    d = D_MODEL
    nc = d // LANES

    @pl.when(pl.program_id(1) == 0)
    def _():
        ubuf_ref[:, 0:HALO, :] = jnp.zeros((nc, HALO, LANES), F32)

    x = x_ref[0]
    xn = _rms(x, ng_ref[...]).astype(BF16)
    a = _dot(xn, w1_ref[...]) + b1_ref[...]
    u = a[:, :d] * jax.nn.sigmoid(a[:, d:])
    for c in range(nc):
        ubuf_ref[c, HALO:HALO + tm, :] = u[:, c * LANES:(c + 1) * LANES]

    def col_body(c, carry):
        for r0 in range(0, tm, rc):
            acc = None
            for k in range(CONV_WIDTH):
                seg = ubuf_ref[c, pl.ds(r0 + k + HALO - (CONV_WIDTH - 1), rc), :]
                t = seg * dw_ref[c, k:k + 1, :]
                acc = t if acc is None else acc + t
            ybuf_ref[c, r0:r0 + rc, :] = acc
        ubuf_ref[c, 0:HALO, :] = ubuf_ref[c, tm:tm + HALO, :]
        return carry

    lax.fori_loop(0, nc, col_body, 0)

    y = jnp.concatenate([ybuf_ref[c] for c in range(nc)], axis=-1) + dwb_ref[...]
    mu = jnp.mean(y, axis=-1, keepdims=True)
    yc = y - mu
    var = jnp.mean(yc * yc, axis=-1, keepdims=True)
    yn = yc * lax.rsqrt(var + EPS) * lng_ref[...] + lnb_ref[...]
    z = (yn * jax.nn.sigmoid(yn)).astype(BF16)
    o_ref[0] = x + _dot(z, w2_ref[...]) + b2_ref[...]


def _conv_mixer(x, ng, w1, b1, dw, dwb, lng, lnb, w2, b2, *, tm=256, rc=64):
    b, s, d = x.shape
    nc = d // LANES
    dw_p = jnp.pad(dw, ((0, HALO - CONV_WIDTH), (0, 0))).reshape(HALO, nc, LANES).transpose(1, 0, 2)
    row = lambda v: v.reshape(1, -1).astype(F32)
    kern = functools.partial(_conv_mixer_kernel, tm=tm, rc=rc)
    return pl.pallas_call(
        kern,
        out_shape=jax.ShapeDtypeStruct((b, s, d), F32),
        grid=(b, s // tm),
        in_specs=[
            pl.BlockSpec((1, tm, d), lambda i, j: (i, j, 0)),
            _const_spec((1, d)), _const_spec((d, 2 * d)), _const_spec((1, 2 * d)),
            _const_spec((nc, HALO, LANES)), _const_spec((1, d)), _const_spec((1, d)),
            _const_spec((1, d)), _const_spec((d, d)), _const_spec((1, d)),
        ],
        out_specs=pl.BlockSpec((1, tm, d), lambda i, j: (i, j, 0)),
        scratch_shapes=[pltpu.VMEM((nc, HALO + tm, LANES), F32), pltpu.VMEM((nc, tm, LANES), F32)],
        compiler_params=_params(("arbitrary", "arbitrary"), 40),
        name="conv_mixer",
    )(x, row(ng), w1.astype(BF16), row(b1), dw_p, row(dwb), row(lng), row(lnb), w2.astype(BF16), row(b2))


def _ffn_chunks(d_ff):
    step = 2 * MXU_DIM
    return [(c0, min(c0 + step, d_ff)) for c0 in range(0, d_ff, step)]


def _ffn_ple_kernel(*refs, has_pre):
    if has_pre:
        (h_ref, on_ref, wo_ref, fg_ref, wg_ref, wu_ref, wd_ref, p_ref, pg_ref, wpg_ref, wpp_ref,
         o_ref) = refs
        h = h_ref[...] + _dot(on_ref[...], wo_ref[...])
    else:
        (h_ref, fg_ref, wg_ref, wu_ref, wd_ref, p_ref, pg_ref, wpg_ref, wpp_ref, o_ref) = refs
        h = h_ref[...]
    xn = _rms(h, fg_ref[...]).astype(BF16)
    acc = None
    for c0, c1 in _ffn_chunks(wg_ref.shape[1]):
        g = _dot(xn, wg_ref[:, c0:c1])
        u = _dot(xn, wu_ref[:, c0:c1])
        a = (g * jax.nn.sigmoid(g) * u).astype(BF16)
        t = _dot(a, wd_ref[c0:c1, :])
        acc = t if acc is None else acc + t
    h2 = h + acc
    gate = jax.nn.sigmoid(_dot(_rms(h2, pg_ref[...]).astype(BF16), wpg_ref[...]))
    proj = _dot(p_ref[...].astype(BF16), wpp_ref[...])
    o_ref[...] = h2 + gate * proj


def _ffn_ple(h, p, layer, fg, wg, wu, wd, pg, wpg, wpp, pre=None, *, tm=512):
    n, d = h.shape
    p_blk = pl.BlockSpec((tm, PLE_DIM), lambda i: (i + layer * (n // tm), 0))
    d_ff = wg.shape[1]
    row = lambda v: v.reshape(1, -1).astype(F32)
    tok = lambda w: pl.BlockSpec((tm, w), lambda i: (i, 0))
    args, specs = [h], [tok(d)]
    if pre is not None:
        on, wo = pre
        args += [on, wo.astype(BF16)]
        specs += [tok(d), _const_spec((d, d))]
    args += [row(fg), wg.astype(BF16), wu.astype(BF16), wd.astype(BF16), p, row(pg),
             wpg.astype(BF16), wpp.astype(BF16)]
    specs += [_const_spec((1, d)), _const_spec((d, d_ff)), _const_spec((d, d_ff)),
              _const_spec((d_ff, d)), p_blk, _const_spec((1, d)), _const_spec((d, d)),
              _const_spec((PLE_DIM, d))]
    return pl.pallas_call(
        functools.partial(_ffn_ple_kernel, has_pre=pre is not None),
        out_shape=jax.ShapeDtypeStruct((n, d), F32),
        grid=(n // tm,),
        in_specs=specs,
        out_specs=tok(d),
        compiler_params=_params(("parallel",), 56),
        name="ffn_ple_pre" if pre is not None else "ffn_ple",
    )(*args)


def _qkv_kernel(h_ref, ng_ref, wqt_ref, wk_ref, wvt_ref, bd_ref, qg_ref, kg_ref, qt_ref, k_ref, vt_ref):
    d = D_MODEL
    tm = h_ref.shape[0]
    xn = _rms(h_ref[...], ng_ref[...]).astype(BF16)
    nt = (((1,), (1,)), ((), ()))

    qt = lax.dot_general(wqt_ref[...], xn, nt, preferred_element_type=F32).reshape(d // HEAD_DIM, HEAD_DIM, tm)
    qt = qt * lax.rsqrt(jnp.mean(qt * qt, axis=1, keepdims=True) + EPS)
    gain = jnp.concatenate([qg_ref[...]] * (tm // LANES), axis=1)
    qt_ref[0] = (qt.reshape(d, tm) * gain).astype(BF16)

    k = _dot(xn, wk_ref[...])
    sq = (k * k).astype(BF16)
    ms = jnp.concatenate([_dot(sq[:, c:c + MXU_DIM], bd_ref[...]) for c in range(0, d, MXU_DIM)], axis=-1)
    k_ref[...] = (k * lax.rsqrt(ms + EPS) * kg_ref[...]).astype(BF16)

    vt_ref[0] = lax.dot_general(wvt_ref[...], xn, nt, preferred_element_type=F32).astype(BF16)


def _qkv_proj(h, ng, w, qg, kg, *, tm):
    n, d = h.shape
    reps = d // HEAD_DIM
    grp = jnp.arange(MXU_DIM) // HEAD_DIM
    bd = jnp.where(grp[:, None] == grp[None, :], 1.0 / HEAD_DIM, 0.0).astype(BF16)
    qg_t = jnp.tile(qg.astype(F32), reps) * (HEAD_DIM ** -0.5 * LOG2E)
    qg_t = jnp.broadcast_to(qg_t[:, None], (d, LANES))
    kg_t = jnp.tile(kg.astype(F32), reps).reshape(1, d)
    wb = w.astype(BF16)
    wqt, wk, wvt = wb[:, :d].T, wb[:, d:2 * d], wb[:, 2 * d:].T
    tok = pl.BlockSpec((tm, d), lambda i: (i, 0))
    tr = pl.BlockSpec((1, d, tm), lambda i: (i, 0, 0))
    tr_shape = jax.ShapeDtypeStruct((n // tm, d, tm), BF16)
    return pl.pallas_call(
        _qkv_kernel,
        out_shape=(tr_shape, jax.ShapeDtypeStruct((n, d), BF16), tr_shape),
        grid=(n // tm,),
        in_specs=[tok, _const_spec((1, d)), _const_spec((d, d)), _const_spec((d, d)), _const_spec((d, d)),
                  _const_spec((MXU_DIM, MXU_DIM)), _const_spec((d, LANES)), _const_spec((1, d))],
        out_specs=(tr, tok, tr),
        compiler_params=_params(("parallel",), 48),
        name="qkv_proj",
    )(h, ng.reshape(1, d).astype(F32), wqt, wk, wvt, bd, qg_t, kg_t)


def _attn_prep_kernel(rb_ref, lq1_ref, lk1_ref, lq2_ref, lk2_ref, tab_ref, lam_ref,
                      *, t, lambda_init):
    h = pl.program_id(0)
    far = rb_ref[NUM_BUCKETS - 1, h]
    blk = MAX_DISTANCE
    nb = t // blk
    skew = (lax.broadcasted_iota(jnp.int32, (blk, blk), 1) - lax.broadcasted_iota(jnp.int32, (blk, blk), 0))

    def bias_of(dd):
        n = jnp.maximum(dd, 0)
        nf = jnp.maximum(n, 1).astype(F32)
        large = MAX_EXACT + (jnp.log(nf / MAX_EXACT) / math.log(MAX_DISTANCE / MAX_EXACT)
                             * (NUM_BUCKETS - MAX_EXACT)).astype(jnp.int32)
        large = jnp.minimum(large, NUM_BUCKETS - 1)
        bucket = jnp.where(n < MAX_EXACT, n, large)
        out = jnp.zeros(dd.shape, F32)
        for bk in range(NUM_BUCKETS):
            out = jnp.where(bucket == bk, rb_ref[bk, h], out)
        return (out - far) * LOG2E

    def tile(jb, ib):
        return (slice(jb * blk, (jb + 1) * blk), slice(ib * blk, (ib + 1) * blk))

    zeros = jnp.zeros((t, t), F32)
    masked = jnp.full((blk, blk), NEG_BIG, F32)
    tab_ref[0, 0] = zeros
    for jb in range(nb):
        for ib in range(jb):
            tab_ref[(0, 0) + tile(jb, ib)] = masked
        tab_ref[(0, 0) + tile(jb, jb)] = jnp.where(skew >= 0, bias_of(skew), NEG_BIG)
        if jb + 1 < nb:
            tab_ref[(0, 0) + tile(jb, jb + 1)] = bias_of(skew + blk)
    tab_ref[0, 1] = zeros
    tab_ref[(0, 1) + tile(nb - 1, 0)] = bias_of(skew + blk)
    tab_ref[0, 2] = zeros
    tab_ref[0, 3] = jnp.full((t, t), NEG_BIG, F32)
    lam = (jnp.exp(jnp.sum(lq1_ref[...] * lk1_ref[...])) - jnp.exp(jnp.sum(lq2_ref[...] * lk2_ref[...]))
           + lambda_init)
    lam_ref[...] = jnp.full(lam_ref.shape, lam, F32)


def _attn_prep(rel_bias, lq1, lk1, lq2, lk2, *, t, lambda_init):
    row = lambda v: v.reshape(1, -1).astype(F32)
    vec = _const_spec((1, HEAD_DIM))
    tab = jax.ShapeDtypeStruct((N_HEADS, 4, t, t), F32)
    return pl.pallas_call(
        functools.partial(_attn_prep_kernel, t=t, lambda_init=lambda_init),
        out_shape=(tab, jax.ShapeDtypeStruct((8, LANES), F32)),
        grid=(N_HEADS,),
        in_specs=[pl.BlockSpec(memory_space=pltpu.SMEM), vec, vec, vec, vec],
        out_specs=(pl.BlockSpec((1, 4, t, t), lambda h: (h, 0, 0, 0)), pl.BlockSpec((8, LANES), lambda h: (0, 0))),
        compiler_params=_params(("arbitrary",), 32),
        name="attn_prep",
    )(rel_bias.astype(F32), row(lq1), row(lk1), row(lq2), row(lk2))


ONES_ROWS = 16
SUBLANES = 8


def _attn_kernel(qt_ref, k_ref, vt_ref, tab_ref, lam_ref, sg_ref, o_ref,
                 qst_ref, sa_ref, sb_ref, rma_ref, rmb_ref, m_ref, acc_ref, *, t, nq):
    r = 2 * t
    sub = lax.broadcasted_iota(jnp.int32, (V_DIM, t), 0)
    ones = jnp.ones((ONES_ROWS, t), BF16)
    assert nq >= 3

    def build_queries(qi):
        qt = qt_ref[0, qi]
        zero = jnp.zeros_like(qt)
        slot = lax.rem(qi, 2)
        qst_ref[slot, :, 0:t] = jnp.where(sub < HEAD_DIM, qt, zero)
        qst_ref[slot, :, t:r] = jnp.where(sub >= HEAD_DIM, qt, zero)

    col_blocks = [slice(j, j + MXU_DIM) for j in range(0, r, MXU_DIM)]

    def logits_block(qi, c, table, cols, s_ref, rm_ref):
        k = k_ref[0, pl.ds(pl.multiple_of(c * t, t), t), :]
        s = _dot(k, qst_ref[lax.rem(qi, 2), :, cols])
        if table is not None:
            bias_cols = slice(cols.start % t, cols.start % t + MXU_DIM)
            s = s + tab_ref[0, table, :, bias_cols]
        s_ref[:, cols] = s
        rm_ref[:, cols] = jnp.broadcast_to(jnp.max(s, axis=0, keepdims=True), (SUBLANES, MXU_DIM))

    def accumulate_block(qi, c, first, cols, s_ref, rm_ref):
        m_new = rm_ref[:, cols] if first else jnp.maximum(m_ref[qi, :, cols], rm_ref[:, cols])
        p = jnp.exp2(s_ref[:, cols] - jnp.tile(m_new, (t // SUBLANES, 1))).astype(BF16)
        pv = _dot(jnp.concatenate([vt_ref[0, c], ones], axis=0), p)
        if not first:
            alpha = jnp.exp2(m_ref[qi, :, cols] - m_new)
            pv = jnp.tile(alpha, (acc_ref.shape[1] // SUBLANES, 1)) * acc_ref[qi, :, cols] + pv
        acc_ref[qi, :, cols] = pv
        m_ref[qi, :, cols] = m_new

    def step(cur, first, nxt, table, buf_cur, buf_nxt):
        for cols in col_blocks:
            if nxt is not None:
                logits_block(*nxt, table, cols, *buf_nxt)
            accumulate_block(*cur, first, cols, *buf_cur)

    buf_a = (sa_ref, rma_ref)
    buf_b = (sb_ref, rmb_ref)
    build_queries(0)
    for cols in col_blocks:
        logits_block(0, 0, 0, cols, *buf_a)

    def near_block(qi):
        end = qi == nq - 1
        q_up = jnp.where(end, 2, qi + 1)
        c_prev = jnp.maximum(qi - 1, 0)
        build_queries(q_up)
        step((qi, qi), True, (qi, c_prev), jnp.where(qi == 0, 3, 1), buf_a, buf_b)
        step((qi, c_prev), False, (q_up, jnp.where(end, 0, q_up)), jnp.where(end, 2, 0), buf_b, buf_a)

    near_unroll = 4
    assert nq % near_unroll == 0

    def near_trip(i, carry):
        for j in range(near_unroll):
            near_block(near_unroll * i + j)
        return carry

    lax.fori_loop(0, nq // near_unroll, near_trip, 0)

    def following(qi, c):
        wrap = c == 0
        qn = jnp.where(wrap, jnp.minimum(qi + 1, nq - 1), qi)
        return qn, jnp.where(wrap, qn - 2, c - 1)

    def far_step(cur, buf_cur, buf_nxt):
        qi = cur[0]
        build_queries(jnp.where(qi + 1 < nq, qi + 1, qi - 1))
        nxt = following(*cur)
        step(cur, False, nxt, None, buf_cur, buf_nxt)
        return nxt

    far_unroll = 26

    def far_trip(i, cur):
        for _ in range(far_unroll // 2):
            cur = far_step(far_step(cur, buf_a, buf_b), buf_b, buf_a)
        return cur

    n_far = (nq - 1) * (nq - 2) // 2
    assert n_far % far_unroll == 1
    cur = lax.fori_loop(0, n_far // far_unroll, far_trip, (jnp.int32(2), jnp.int32(0)))
    step(cur, False, None, None, buf_a, buf_b)

    lam = jnp.tile(lam_ref[...], (V_DIM // SUBLANES, t // LANES))
    gain = jnp.concatenate([sg_ref[...]] * (t // LANES), axis=1)

    def finish_block(qi):
        acc = acc_ref[qi]
        inv = 1.0 / acc[V_DIM:V_DIM + SUBLANES]
        o_all = acc[0:V_DIM] * jnp.tile(inv, (V_DIM // SUBLANES, 1))
        o = o_all[:, 0:t] - lam * o_all[:, t:r]
        o = o * lax.rsqrt(jnp.mean(o * o, axis=0, keepdims=True) + EPS) * gain
        o_ref[0, pl.ds(pl.multiple_of(qi * t, t), t), :] = o.T.astype(BF16)

    def finish(i, carry):
        finish_block(2 * i)
        finish_block(2 * i + 1)
        return carry

    lax.fori_loop(0, nq // 2, finish, 0)


def _attention(qt, k, vt, tab, lam, sub_g, *, t, out_scale):
    b, s, d = k.shape
    nq = s // t
    rowmajor = pl.BlockSpec((1, s, V_DIM), lambda bi, h: (bi, 0, h))
    featmajor = pl.BlockSpec((1, nq, V_DIM, t), lambda bi, h: (bi, 0, h, 0))
    return pl.pallas_call(
        functools.partial(_attn_kernel, t=t, nq=nq),
        out_shape=jax.ShapeDtypeStruct((b, s, d), BF16),
        grid=(b, N_HEADS),
        in_specs=[featmajor, rowmajor, featmajor, pl.BlockSpec((1, 4, t, t), lambda bi, h: (h, 0, 0, 0)),
                  _const_spec((8, LANES)), _const_spec((V_DIM, LANES))],
        out_specs=rowmajor,
        scratch_shapes=[pltpu.VMEM((2, V_DIM, 2 * t), BF16),
                        pltpu.VMEM((t, 2 * t), F32), pltpu.VMEM((t, 2 * t), F32),
                        pltpu.VMEM((SUBLANES, 2 * t), F32), pltpu.VMEM((SUBLANES, 2 * t), F32),
                        pltpu.VMEM((nq, SUBLANES, 2 * t), F32),
                        pltpu.VMEM((nq, V_DIM + ONES_ROWS, 2 * t), F32)],
        compiler_params=_params(("parallel", "parallel"), 58),
        name="diff_attention",
    )(qt, k, vt, tab, lam, jnp.broadcast_to((sub_g.astype(F32) * out_scale)[:, None], (V_DIM, LANES)))


def kernel(x, p, conv_norm_g, conv_w_pw1, conv_b_pw1, conv_dw_w, conv_dw_b, conv_ln_g, conv_ln_b, conv_w_pw2, conv_b_pw2, attn_norm_g, attn_w_qkv, attn_q_norm_g, attn_k_norm_g, attn_lambda_q1, attn_lambda_k1, attn_lambda_q2, attn_lambda_k2, attn_sub_norm_g, attn_w_o, rel_bias, ffn_norm_g, ffn_w_gate, ffn_w_up, ffn_w_down, ple_norm_g, ple_w_gate, ple_w_proj):
    b, s, d = x.shape
    n = b * s
    t_attn = 512

    h = _conv_mixer(x, conv_norm_g[0], conv_w_pw1[0], conv_b_pw1[0], conv_dw_w[0], conv_dw_b[0],
                    conv_ln_g[0], conv_ln_b[0], conv_w_pw2[0], conv_b_pw2[0])
    p2 = p.reshape(-1, PLE_DIM)
    h = _ffn_ple(h.reshape(n, d), p2, 0, ffn_norm_g[0], ffn_w_gate[0], ffn_w_up[0],
                 ffn_w_down[0], ple_norm_g[0], ple_w_gate[0], ple_w_proj[0])

    lambda_init = 0.8 - 0.6 * math.exp(-0.3 * 1)
    qt, k, vt = _qkv_proj(h, attn_norm_g[0], attn_w_qkv[0], attn_q_norm_g[0], attn_k_norm_g[0], tm=t_attn)
    nq = s // t_attn
    tab, lam = _attn_prep(rel_bias, attn_lambda_q1[0], attn_lambda_k1[0], attn_lambda_q2[0],
                          attn_lambda_k2[0], t=t_attn, lambda_init=lambda_init)
    on = _attention(qt.reshape(b, nq, d, t_attn), k.reshape(b, s, d), vt.reshape(b, nq, d, t_attn), tab, lam,
                    attn_sub_norm_g[0], t=t_attn, out_scale=1.0 - lambda_init)
    h = _ffn_ple(h, p2, 1, ffn_norm_g[1], ffn_w_gate[1], ffn_w_up[1], ffn_w_down[1],
                 ple_norm_g[1], ple_w_gate[1], ple_w_proj[1], pre=(on.reshape(n, d), attn_w_o[0]))
    return h.reshape(b, s, d)
```

```python
import functools
import math

import jax
import jax.numpy as jnp
from jax import lax
from jax.experimental import pallas as pl
from jax.experimental.pallas import tpu as pltpu

D_MODEL = 1024
CONV_WIDTH = 31
N_HEADS = 8
HEAD_DIM = 64
V_DIM = 128
NUM_BUCKETS = 32
MAX_EXACT = 16
MAX_DISTANCE = 128
PLE_DIM = 256
EPS = 1e-6

LANES = 128
MXU_DIM = 256
VMEM_BYTES = 64 * 2 ** 20
HALO = 32
NEG_BIG = -1e30
LOG2E = math.log2(math.e)

F32 = jnp.float32
BF16 = jnp.bfloat16


def _rms(x, g):
    return x * lax.rsqrt(jnp.mean(x * x, axis=-1, keepdims=True) + EPS) * g


def _dot(a, b):
    return jnp.dot(a, b, preferred_element_type=F32)


def _const_spec(shape):
    nd = len(shape)
    return pl.BlockSpec(shape, lambda *_: (0,) * nd, pipeline_mode=pl.Buffered(1))


def _params(semantics, vmem_mib, flags=None):
    return pltpu.CompilerParams(dimension_semantics=semantics, flags=flags,
                                vmem_limit_bytes=min(vmem_mib * 2 ** 20, VMEM_BYTES - 6 * 2 ** 20))


def _conv_mixer_kernel(x_ref, ng_ref, w1_ref, b1_ref, dw_ref, dwb_ref, lng_ref, lnb_ref,
                       w2_ref, b2_ref, o_ref, ubuf_ref, ybuf_ref, *, tm, rc):
    d = D_MODEL
    nc = d // LANES

    @pl.when(pl.program_id(1) == 0)
    def _():
        ubuf_ref[:, 0:HALO, :] = jnp.zeros((nc, HALO, LANES), F32)

    x = x_ref[0]
    xn = _rms(x, ng_ref[...]).astype(BF16)
    a = _dot(xn, w1_ref[...]) + b1_ref[...]
    u = a[:, :d] * jax.nn.sigmoid(a[:, d:])
    for c in range(nc):
        ubuf_ref[c, HALO:HALO + tm, :] = u[:, c * LANES:(c + 1) * LANES]

    def col_body(c, carry):
        for r0 in range(0, tm, rc):
            acc = None
            for k in range(CONV_WIDTH):
                seg = ubuf_ref[c, pl.ds(r0 + k + HALO - (CONV_WIDTH - 1), rc), :]
                t = seg * dw_ref[c, k:k + 1, :]
                acc = t if acc is None else acc + t
            ybuf_ref[c, r0:r0 + rc, :] = acc
        ubuf_ref[c, 0:HALO, :] = ubuf_ref[c, tm:tm + HALO, :]
        return carry

    lax.fori_loop(0, nc, col_body, 0)

    y = jnp.concatenate([ybuf_ref[c] for c in range(nc)], axis=-1) + dwb_ref[...]
    mu = jnp.mean(y, axis=-1, keepdims=True)
    yc = y - mu
    var = jnp.mean(yc * yc, axis=-1, keepdims=True)
    yn = yc * lax.rsqrt(var + EPS) * lng_ref[...] + lnb_ref[...]
    z = (yn * jax.nn.sigmoid(yn)).astype(BF16)
    o_ref[0] = x + _dot(z, w2_ref[...]) + b2_ref[...]


def _conv_mixer(x, ng, w1, b1, dw, dwb, lng, lnb, w2, b2, *, tm=256, rc=64):
    b, s, d = x.shape
    nc = d // LANES
    dw_p = jnp.pad(dw, ((0, HALO - CONV_WIDTH), (0, 0))).reshape(HALO, nc, LANES).transpose(1, 0, 2)
    row = lambda v: v.reshape(1, -1).astype(F32)
    kern = functools.partial(_conv_mixer_kernel, tm=tm, rc=rc)
    return pl.pallas_call(
        kern,
        out_shape=jax.ShapeDtypeStruct((b, s, d), F32),
        grid=(b, s // tm),
        in_specs=[
            pl.BlockSpec((1, tm, d), lambda i, j: (i, j, 0)),
            _const_spec((1, d)), _const_spec((d, 2 * d)), _const_spec((1, 2 * d)),
            _const_spec((nc, HALO, LANES)), _const_spec((1, d)), _const_spec((1, d)),
            _const_spec((1, d)), _const_spec((d, d)), _const_spec((1, d)),
        ],
        out_specs=pl.BlockSpec((1, tm, d), lambda i, j: (i, j, 0)),
        scratch_shapes=[pltpu.VMEM((nc, HALO + tm, LANES), F32), pltpu.VMEM((nc, tm, LANES), F32)],
        compiler_params=_params(("arbitrary", "arbitrary"), 40),
        name="conv_mixer",
    )(x, row(ng), w1.astype(BF16), row(b1), dw_p, row(dwb), row(lng), row(lnb), w2.astype(BF16), row(b2))


def _ffn_chunks(d_ff):
    step = 2 * MXU_DIM
    return [(c0, min(c0 + step, d_ff)) for c0 in range(0, d_ff, step)]


def _ffn_ple_kernel(*refs, has_pre):
    if has_pre:
        (h_ref, on_ref, wo_ref, fg_ref, wg_ref, wu_ref, wd_ref, p_ref, pg_ref, wpg_ref, wpp_ref,
         o_ref) = refs
        h = h_ref[...] + _dot(on_ref[...], wo_ref[...])
    else:
        (h_ref, fg_ref, wg_ref, wu_ref, wd_ref, p_ref, pg_ref, wpg_ref, wpp_ref, o_ref) = refs
        h = h_ref[...]
    xn = _rms(h, fg_ref[...]).astype(BF16)
    acc = None
    for c0, c1 in _ffn_chunks(wg_ref.shape[1]):
        g = _dot(xn, wg_ref[:, c0:c1])
        u = _dot(xn, wu_ref[:, c0:c1])
        a = (g * jax.nn.sigmoid(g) * u).astype(BF16)
        t = _dot(a, wd_ref[c0:c1, :])
        acc = t if acc is None else acc + t
    h2 = h + acc
    gate = jax.nn.sigmoid(_dot(_rms(h2, pg_ref[...]).astype(BF16), wpg_ref[...]))
    proj = _dot(p_ref[...].astype(BF16), wpp_ref[...])
    o_ref[...] = h2 + gate * proj


def _ffn_ple(h, p, layer, fg, wg, wu, wd, pg, wpg, wpp, pre=None, *, tm=512):
    n, d = h.shape
    p_blk = pl.BlockSpec((tm, PLE_DIM), lambda i: (i + layer * (n // tm), 0))
    d_ff = wg.shape[1]
    row = lambda v: v.reshape(1, -1).astype(F32)
    tok = lambda w: pl.BlockSpec((tm, w), lambda i: (i, 0))
    args, specs = [h], [tok(d)]
    if pre is not None:
        on, wo = pre
        args += [on, wo.astype(BF16)]
        specs += [tok(d), _const_spec((d, d))]
    args += [row(fg), wg.astype(BF16), wu.astype(BF16), wd.astype(BF16), p, row(pg),
             wpg.astype(BF16), wpp.astype(BF16)]
    specs += [_const_spec((1, d)), _const_spec((d, d_ff)), _const_spec((d, d_ff)),
              _const_spec((d_ff, d)), p_blk, _const_spec((1, d)), _const_spec((d, d)),
              _const_spec((PLE_DIM, d))]
    return pl.pallas_call(
        functools.partial(_ffn_ple_kernel, has_pre=pre is not None),
        out_shape=jax.ShapeDtypeStruct((n, d), F32),
        grid=(n // tm,),
        in_specs=specs,
        out_specs=tok(d),
        compiler_params=_params(("parallel",), 56),
        name="ffn_ple_pre" if pre is not None else "ffn_ple",
    )(*args)


def _qkv_kernel(h_ref, ng_ref, wqt_ref, wk_ref, wvt_ref, bd_ref, qg_ref, kg_ref, qt_ref, k_ref, vt_ref):
    d = D_MODEL
    tm = h_ref.shape[0]
    xn = _rms(h_ref[...], ng_ref[...]).astype(BF16)
    nt = (((1,), (1,)), ((), ()))

    qt = lax.dot_general(wqt_ref[...], xn, nt, preferred_element_type=F32).reshape(d // HEAD_DIM, HEAD_DIM, tm)
    qt = qt * lax.rsqrt(jnp.mean(qt * qt, axis=1, keepdims=True) + EPS)
    gain = jnp.concatenate([qg_ref[...]] * (tm // LANES), axis=1)
    qt_ref[0] = (qt.reshape(d, tm) * gain).astype(BF16)

    k = _dot(xn, wk_ref[...])
    sq = (k * k).astype(BF16)
    ms = jnp.concatenate([_dot(sq[:, c:c + MXU_DIM], bd_ref[...]) for c in range(0, d, MXU_DIM)], axis=-1)
    k_ref[...] = (k * lax.rsqrt(ms + EPS) * kg_ref[...]).astype(BF16)

    vt_ref[0] = lax.dot_general(wvt_ref[...], xn, nt, preferred_element_type=F32).astype(BF16)


def _qkv_proj(h, ng, w, qg, kg, *, tm):
    n, d = h.shape
    reps = d // HEAD_DIM
    grp = jnp.arange(MXU_DIM) // HEAD_DIM
    bd = jnp.where(grp[:, None] == grp[None, :], 1.0 / HEAD_DIM, 0.0).astype(BF16)
    qg_t = jnp.tile(qg.astype(F32), reps) * (HEAD_DIM ** -0.5 * LOG2E)
    qg_t = jnp.broadcast_to(qg_t[:, None], (d, LANES))
    kg_t = jnp.tile(kg.astype(F32), reps).reshape(1, d)
    wb = w.astype(BF16)
    wqt, wk, wvt = wb[:, :d].T, wb[:, d:2 * d], wb[:, 2 * d:].T
    tok = pl.BlockSpec((tm, d), lambda i: (i, 0))
    tr = pl.BlockSpec((1, d, tm), lambda i: (i, 0, 0))
    tr_shape = jax.ShapeDtypeStruct((n // tm, d, tm), BF16)
    return pl.pallas_call(
        _qkv_kernel,
        out_shape=(tr_shape, jax.ShapeDtypeStruct((n, d), BF16), tr_shape),
        grid=(n // tm,),
        in_specs=[tok, _const_spec((1, d)), _const_spec((d, d)), _const_spec((d, d)), _const_spec((d, d)),
                  _const_spec((MXU_DIM, MXU_DIM)), _const_spec((d, LANES)), _const_spec((1, d))],
        out_specs=(tr, tok, tr),
        compiler_params=_params(("parallel",), 48),
        name="qkv_proj",
    )(h, ng.reshape(1, d).astype(F32), wqt, wk, wvt, bd, qg_t, kg_t)


def _attn_prep_kernel(rb_ref, lq1_ref, lk1_ref, lq2_ref, lk2_ref, tab_ref, lam_ref,
                      *, t, lambda_init):
    h = pl.program_id(0)
    far = rb_ref[NUM_BUCKETS - 1, h]
    blk = MAX_DISTANCE
    nb = t // blk
    skew = (lax.broadcasted_iota(jnp.int32, (blk, blk), 1) - lax.broadcasted_iota(jnp.int32, (blk, blk), 0))

    def bias_of(dd):
        n = jnp.maximum(dd, 0)
        nf = jnp.maximum(n, 1).astype(F32)
        large = MAX_EXACT + (jnp.log(nf / MAX_EXACT) / math.log(MAX_DISTANCE / MAX_EXACT)
                             * (NUM_BUCKETS - MAX_EXACT)).astype(jnp.int32)
        large = jnp.minimum(large, NUM_BUCKETS - 1)
        bucket = jnp.where(n < MAX_EXACT, n, large)
        out = jnp.zeros(dd.shape, F32)
        for bk in range(NUM_BUCKETS):
            out = jnp.where(bucket == bk, rb_ref[bk, h], out)
        return (out - far) * LOG2E

    def tile(jb, ib):
        return (slice(jb * blk, (jb + 1) * blk), slice(ib * blk, (ib + 1) * blk))

    zeros = jnp.zeros((t, t), F32)
    masked = jnp.full((blk, blk), NEG_BIG, F32)
    tab_ref[0, 0] = zeros
    for jb in range(nb):
        for ib in range(jb):
            tab_ref[(0, 0) + tile(jb, ib)] = masked
        tab_ref[(0, 0) + tile(jb, jb)] = jnp.where(skew >= 0, bias_of(skew), NEG_BIG)
        if jb + 1 < nb:
            tab_ref[(0, 0) + tile(jb, jb + 1)] = bias_of(skew + blk)
    tab_ref[0, 1] = zeros
    tab_ref[(0, 1) + tile(nb - 1, 0)] = bias_of(skew + blk)
    lam = (jnp.exp(jnp.sum(lq1_ref[...] * lk1_ref[...])) - jnp.exp(jnp.sum(lq2_ref[...] * lk2_ref[...]))
           + lambda_init)
    lam_ref[...] = jnp.full(lam_ref.shape, lam, F32)


def _attn_prep(rel_bias, lq1, lk1, lq2, lk2, *, t, lambda_init):
    row = lambda v: v.reshape(1, -1).astype(F32)
    vec = _const_spec((1, HEAD_DIM))
    tab = jax.ShapeDtypeStruct((N_HEADS, 2, t, t), F32)
    return pl.pallas_call(
        functools.partial(_attn_prep_kernel, t=t, lambda_init=lambda_init),
        out_shape=(tab, jax.ShapeDtypeStruct((8, LANES), F32)),
        grid=(N_HEADS,),
        in_specs=[pl.BlockSpec(memory_space=pltpu.SMEM), vec, vec, vec, vec],
        out_specs=(pl.BlockSpec((1, 2, t, t), lambda h: (h, 0, 0, 0)), pl.BlockSpec((8, LANES), lambda h: (0, 0))),
        compiler_params=_params(("arbitrary",), 32),
        name="attn_prep",
    )(rel_bias.astype(F32), row(lq1), row(lk1), row(lq2), row(lk2))


ONES_ROWS = 16
SUBLANES = 8


def _attn_kernel(qt_ref, k_ref, vt_ref, tab_ref, lam_ref, sg_ref, o_ref,
                 qst_ref, sa_ref, sb_ref, rma_ref, rmb_ref, m_ref, acc_ref, *, t, nq):
    r = 2 * t
    sub = lax.broadcasted_iota(jnp.int32, (V_DIM, t), 0)
    ones = jnp.ones((ONES_ROWS, t), BF16)
    assert nq >= 3

    def build_queries(qi):
        qt = qt_ref[0, qi]
        zero = jnp.zeros_like(qt)
        slot = lax.rem(qi, 2)
        qst_ref[slot, :, 0:t] = jnp.where(sub < HEAD_DIM, qt, zero)
        qst_ref[slot, :, t:r] = jnp.where(sub >= HEAD_DIM, qt, zero)

    col_blocks = [slice(j, j + MXU_DIM) for j in range(0, r, MXU_DIM)]

    def visible_keys(cols, diagonal):
        return min(t, cols.start % t + MXU_DIM) if diagonal else t

    def logits_block(qi, c, table, diagonal, cols, s_ref, rm_ref):
        nk = visible_keys(cols, diagonal)
        k = k_ref[0, pl.ds(pl.multiple_of(c * t, t), nk), :]
        s = _dot(k, qst_ref[lax.rem(qi, 2), :, cols])
        if table is not None:
            bias_cols = slice(cols.start % t, cols.start % t + MXU_DIM)
            s = s + tab_ref[0, table, 0:nk, bias_cols]
        s_ref[0:nk, cols] = s
        rm_ref[:, cols] = jnp.broadcast_to(jnp.max(s, axis=0, keepdims=True), (SUBLANES, MXU_DIM))

    def accumulate_block(qi, c, diagonal, cols, s_ref, rm_ref):
        nk = visible_keys(cols, diagonal)
        m_new = rm_ref[:, cols] if diagonal else jnp.maximum(m_ref[qi, :, cols], rm_ref[:, cols])
        p = jnp.exp2(s_ref[0:nk, cols] - jnp.tile(m_new, (nk // SUBLANES, 1))).astype(BF16)
        pv = _dot(jnp.concatenate([vt_ref[0, c, :, 0:nk], ones[:, 0:nk]], axis=0), p)
        if not diagonal:
            alpha = jnp.exp2(m_ref[qi, :, cols] - m_new)
            pv = jnp.tile(alpha, (acc_ref.shape[1] // SUBLANES, 1)) * acc_ref[qi, :, cols] + pv
        acc_ref[qi, :, cols] = pv
        m_ref[qi, :, cols] = m_new

    def step(cur, cur_diagonal, nxt, table, nxt_diagonal, buf_cur, buf_nxt):
        for cols in col_blocks:
            if nxt is not None:
                logits_block(*nxt, table, nxt_diagonal, cols, *buf_nxt)
            accumulate_block(*cur, cur_diagonal, cols, *buf_cur)

    buf_a = (sa_ref, rma_ref)
    buf_b = (sb_ref, rmb_ref)

    build_queries(0)
    build_queries(1)
    for cols in col_blocks:
        logits_block(0, 0, 0, True, cols, *buf_a)
    step((0, 0), True, (1, 1), 0, True, buf_a, buf_b)

    def near_block(qi, last):
        build_queries(2 if last else qi + 1)
        step((qi, qi), True, (qi, qi - 1), 1, False, buf_b, buf_a)
        if last:
            step((qi, qi - 1), False, (2, 0), None, False, buf_a, buf_b)
        else:
            step((qi, qi - 1), False, (qi + 1, qi + 1), 0, True, buf_a, buf_b)

    near_unroll = 7
    assert (nq - 2) % near_unroll == 0

    def near_trip(i, carry):
        for j in range(near_unroll):
            near_block(1 + near_unroll * i + j, False)
        return carry

    lax.fori_loop(0, (nq - 2) // near_unroll, near_trip, 0)
    near_block(nq - 1, True)

    def following(qi, c):
        wrap = c == 0
        qn = jnp.where(wrap, jnp.minimum(qi + 1, nq - 1), qi)
        return qn, jnp.where(wrap, qn - 2, c - 1)

    def far_step(cur, buf_cur, buf_nxt):
        qi = cur[0]
        build_queries(jnp.where(qi + 1 < nq, qi + 1, qi - 1))
        nxt = following(*cur)
        step(cur, False, nxt, None, False, buf_cur, buf_nxt)
        return nxt

    far_unroll = 26

    def far_trip(i, cur):
        for _ in range(far_unroll // 2):
            cur = far_step(far_step(cur, buf_b, buf_a), buf_a, buf_b)
        return cur

    n_far = (nq - 1) * (nq - 2) // 2
    assert n_far % far_unroll == 1
    cur = lax.fori_loop(0, n_far // far_unroll, far_trip, (jnp.int32(2), jnp.int32(0)))
    step(cur, False, None, None, False, buf_b, buf_a)

    lam = jnp.tile(lam_ref[...], (V_DIM // SUBLANES, t // LANES))
    gain = jnp.concatenate([sg_ref[...]] * (t // LANES), axis=1)

    def finish_block(qi):
        acc = acc_ref[qi]
        inv = 1.0 / acc[V_DIM:V_DIM + SUBLANES]
        o_all = acc[0:V_DIM] * jnp.tile(inv, (V_DIM // SUBLANES, 1))
        o = o_all[:, 0:t] - lam * o_all[:, t:r]
        o = o * lax.rsqrt(jnp.mean(o * o, axis=0, keepdims=True) + EPS) * gain
        o_ref[0, pl.ds(pl.multiple_of(qi * t, t), t), :] = o.T.astype(BF16)

    def finish(i, carry):
        finish_block(2 * i)
        finish_block(2 * i + 1)
        return carry

    lax.fori_loop(0, nq // 2, finish, 0)


def _attention(qt, k, vt, tab, lam, sub_g, *, t, out_scale):
    b, s, d = k.shape
    nq = s // t
    rowmajor = pl.BlockSpec((1, s, V_DIM), lambda bi, h: (bi, 0, h))
    featmajor = pl.BlockSpec((1, nq, V_DIM, t), lambda bi, h: (bi, 0, h, 0))
    return pl.pallas_call(
        functools.partial(_attn_kernel, t=t, nq=nq),
        out_shape=jax.ShapeDtypeStruct((b, s, d), BF16),
        grid=(b, N_HEADS),
        in_specs=[featmajor, rowmajor, featmajor, pl.BlockSpec((1, 2, t, t), lambda bi, h: (h, 0, 0, 0)),
                  _const_spec((8, LANES)), _const_spec((V_DIM, LANES))],
        out_specs=rowmajor,
        scratch_shapes=[pltpu.VMEM((2, V_DIM, 2 * t), BF16),
                        pltpu.VMEM((t, 2 * t), F32), pltpu.VMEM((t, 2 * t), F32),
                        pltpu.VMEM((SUBLANES, 2 * t), F32), pltpu.VMEM((SUBLANES, 2 * t), F32),
                        pltpu.VMEM((nq, SUBLANES, 2 * t), F32),
                        pltpu.VMEM((nq, V_DIM + ONES_ROWS, 2 * t), F32)],
        compiler_params=_params(("parallel", "parallel"), 58),
        name="diff_attention",
    )(qt, k, vt, tab, lam, jnp.broadcast_to((sub_g.astype(F32) * out_scale)[:, None], (V_DIM, LANES)))


def kernel(x, p, conv_norm_g, conv_w_pw1, conv_b_pw1, conv_dw_w, conv_dw_b, conv_ln_g, conv_ln_b, conv_w_pw2, conv_b_pw2, attn_norm_g, attn_w_qkv, attn_q_norm_g, attn_k_norm_g, attn_lambda_q1, attn_lambda_k1, attn_lambda_q2, attn_lambda_k2, attn_sub_norm_g, attn_w_o, rel_bias, ffn_norm_g, ffn_w_gate, ffn_w_up, ffn_w_down, ple_norm_g, ple_w_gate, ple_w_proj):
    b, s, d = x.shape
    n = b * s
    t_attn = 512

    h = _conv_mixer(x, conv_norm_g[0], conv_w_pw1[0], conv_b_pw1[0], conv_dw_w[0], conv_dw_b[0],
                    conv_ln_g[0], conv_ln_b[0], conv_w_pw2[0], conv_b_pw2[0])
    p2 = p.reshape(-1, PLE_DIM)
    h = _ffn_ple(h.reshape(n, d), p2, 0, ffn_norm_g[0], ffn_w_gate[0], ffn_w_up[0],
                 ffn_w_down[0], ple_norm_g[0], ple_w_gate[0], ple_w_proj[0])

    lambda_init = 0.8 - 0.6 * math.exp(-0.3 * 1)
    qt, k, vt = _qkv_proj(h, attn_norm_g[0], attn_w_qkv[0], attn_q_norm_g[0], attn_k_norm_g[0], tm=t_attn)
    nq = s // t_attn
    tab, lam = _attn_prep(rel_bias, attn_lambda_q1[0], attn_lambda_k1[0], attn_lambda_q2[0],
                          attn_lambda_k2[0], t=t_attn, lambda_init=lambda_init)
    on = _attention(qt.reshape(b, nq, d, t_attn), k.reshape(b, s, d), vt.reshape(b, nq, d, t_attn), tab, lam,
                    attn_sub_norm_g[0], t=t_attn, out_scale=1.0 - lambda_init)
    h = _ffn_ple(h, p2, 1, ffn_norm_g[1], ffn_w_gate[1], ffn_w_up[1], ffn_w_down[1],
                 ple_norm_g[1], ple_w_gate[1], ple_w_proj[1], pre=(on.reshape(n, d), attn_w_o[0]))
    return h.reshape(b, s, d)
```

```python
import functools
import math

import jax
import jax.numpy as jnp
from jax import lax
from jax.experimental import pallas as pl
from jax.experimental.pallas import tpu as pltpu

D_MODEL = 1024
CONV_WIDTH = 31
N_HEADS = 8
HEAD_DIM = 64
V_DIM = 128
NUM_BUCKETS = 32
MAX_EXACT = 16
MAX_DISTANCE = 128
PLE_DIM = 256
EPS = 1e-6

LANES = 128
MXU_DIM = 256
VMEM_BYTES = 64 * 2 ** 20
HALO = 32
NEG_BIG = -1e30
LOG2E = math.log2(math.e)

F32 = jnp.float32
BF16 = jnp.bfloat16


def _rms(x, g):
    return x * lax.rsqrt(jnp.mean(x * x, axis=-1, keepdims=True) + EPS) * g


def _dot(a, b):
    return jnp.dot(a, b, preferred_element_type=F32)


def _const_spec(shape):
    nd = len(shape)
    return pl.BlockSpec(shape, lambda *_: (0,) * nd, pipeline_mode=pl.Buffered(1))


def _params(semantics, vmem_mib, flags=None):
    return pltpu.CompilerParams(dimension_semantics=semantics, flags=flags,
                                vmem_limit_bytes=min(vmem_mib * 2 ** 20, VMEM_BYTES - 6 * 2 ** 20))


def _conv_mixer_kernel(x_ref, ng_ref, w1_ref, b1_ref, dw_ref, dwb_ref, lng_ref, lnb_ref,
                       w2_ref, b2_ref, o_ref, ubuf_ref, ybuf_ref, *, tm, rc):
    d = D_MODEL
    nc = d // LANES

    @pl.when(pl.program_id(1) == 0)
    def _():
        ubuf_ref[:, 0:HALO, :] = jnp.zeros((nc, HALO, LANES), F32)

    x = x_ref[0]
    xn = _rms(x, ng_ref[...]).astype(BF16)
    a = _dot(xn, w1_ref[...]) + b1_ref[...]
    u = a[:, :d] * jax.nn.sigmoid(a[:, d:])
    for c in range(nc):
        ubuf_ref[c, HALO:HALO + tm, :] = u[:, c * LANES:(c + 1) * LANES]

    def col_body(c, carry):
        for r0 in range(0, tm, rc):
            acc = None
            for k in range(CONV_WIDTH):
                seg = ubuf_ref[c, pl.ds(r0 + k + HALO - (CONV_WIDTH - 1), rc), :]
                t = seg * dw_ref[c, k:k + 1, :]
                acc = t if acc is None else acc + t
            ybuf_ref[c, r0:r0 + rc, :] = acc
        ubuf_ref[c, 0:HALO, :] = ubuf_ref[c, tm:tm + HALO, :]
        return carry

    lax.fori_loop(0, nc, col_body, 0)

    y = jnp.concatenate([ybuf_ref[c] for c in range(nc)], axis=-1) + dwb_ref[...]
    mu = jnp.mean(y, axis=-1, keepdims=True)
    yc = y - mu
    var = jnp.mean(yc * yc, axis=-1, keepdims=True)
    yn = yc * lax.rsqrt(var + EPS) * lng_ref[...] + lnb_ref[...]
    z = (yn * jax.nn.sigmoid(yn)).astype(BF16)
    o_ref[0] = x + _dot(z, w2_ref[...]) + b2_ref[...]


def _conv_mixer(x, ng, w1, b1, dw, dwb, lng, lnb, w2, b2, *, tm=512, rc=64):
    b, s, d = x.shape
    nc = d // LANES
    dw_p = jnp.pad(dw, ((0, HALO - CONV_WIDTH), (0, 0))).reshape(HALO, nc, LANES).transpose(1, 0, 2)
    row = lambda v: v.reshape(1, -1).astype(F32)
    kern = functools.partial(_conv_mixer_kernel, tm=tm, rc=rc)
    return pl.pallas_call(
        kern,
        out_shape=jax.ShapeDtypeStruct((b, s, d), F32),
        grid=(b, s // tm),
        in_specs=[
            pl.BlockSpec((1, tm, d), lambda i, j: (i, j, 0)),
            _const_spec((1, d)), _const_spec((d, 2 * d)), _const_spec((1, 2 * d)),
            _const_spec((nc, HALO, LANES)), _const_spec((1, d)), _const_spec((1, d)),
            _const_spec((1, d)), _const_spec((d, d)), _const_spec((1, d)),
        ],
        out_specs=pl.BlockSpec((1, tm, d), lambda i, j: (i, j, 0)),
        scratch_shapes=[pltpu.VMEM((nc, HALO + tm, LANES), F32), pltpu.VMEM((nc, tm, LANES), F32)],
        compiler_params=_params(("arbitrary", "arbitrary"), 40),
        name="conv_mixer",
    )(x, row(ng), w1.astype(BF16), row(b1), dw_p, row(dwb), row(lng), row(lnb), w2.astype(BF16), row(b2))


def _ffn_chunks(d_ff):
    step = 2 * MXU_DIM
    return [(c0, min(c0 + step, d_ff)) for c0 in range(0, d_ff, step)]


def _ffn_ple_kernel(*refs, has_pre):
    if has_pre:
        (h_ref, on_ref, wo_ref, fg_ref, wg_ref, wu_ref, wd_ref, p_ref, pg_ref, wpg_ref, wpp_ref,
         o_ref) = refs
        h = h_ref[...] + _dot(on_ref[...], wo_ref[...])
    else:
        (h_ref, fg_ref, wg_ref, wu_ref, wd_ref, p_ref, pg_ref, wpg_ref, wpp_ref, o_ref) = refs
        h = h_ref[...]
    xn = _rms(h, fg_ref[...]).astype(BF16)
    acc = None
    for c0, c1 in _ffn_chunks(wg_ref.shape[1]):
        g = _dot(xn, wg_ref[:, c0:c1])
        u = _dot(xn, wu_ref[:, c0:c1])
        a = (g * jax.nn.sigmoid(g) * u).astype(BF16)
        t = _dot(a, wd_ref[c0:c1, :])
        acc = t if acc is None else acc + t
    h2 = h + acc
    gate = jax.nn.sigmoid(_dot(_rms(h2, pg_ref[...]).astype(BF16), wpg_ref[...]))
    proj = _dot(p_ref[...].astype(BF16), wpp_ref[...])
    o_ref[...] = h2 + gate * proj


def _ffn_ple(h, p, layer, fg, wg, wu, wd, pg, wpg, wpp, pre=None, *, tm=512):
    n, d = h.shape
    p_blk = pl.BlockSpec((tm, PLE_DIM), lambda i: (i + layer * (n // tm), 0))
    d_ff = wg.shape[1]
    row = lambda v: v.reshape(1, -1).astype(F32)
    tok = lambda w: pl.BlockSpec((tm, w), lambda i: (i, 0))
    args, specs = [h], [tok(d)]
    if pre is not None:
        on, wo = pre
        args += [on, wo.astype(BF16)]
        specs += [tok(d), _const_spec((d, d))]
    args += [row(fg), wg.astype(BF16), wu.astype(BF16), wd.astype(BF16), p, row(pg),
             wpg.astype(BF16), wpp.astype(BF16)]
    specs += [_const_spec((1, d)), _const_spec((d, d_ff)), _const_spec((d, d_ff)),
              _const_spec((d_ff, d)), p_blk, _const_spec((1, d)), _const_spec((d, d)),
              _const_spec((PLE_DIM, d))]
    return pl.pallas_call(
        functools.partial(_ffn_ple_kernel, has_pre=pre is not None),
        out_shape=jax.ShapeDtypeStruct((n, d), F32),
        grid=(n // tm,),
        in_specs=specs,
        out_specs=tok(d),
        compiler_params=_params(("parallel",), 56),
        name="ffn_ple_pre" if pre is not None else "ffn_ple",
    )(*args)


def _qkv_kernel(h_ref, ng_ref, wqt_ref, wk_ref, wvt_ref, bd_ref, qg_ref, kg_ref, qt_ref, k_ref, vt_ref):
    d = D_MODEL
    tm = h_ref.shape[0]
    xn = _rms(h_ref[...], ng_ref[...]).astype(BF16)
    nt = (((1,), (1,)), ((), ()))

    qt = lax.dot_general(wqt_ref[...], xn, nt, preferred_element_type=F32).reshape(d // HEAD_DIM, HEAD_DIM, tm)
    qt = qt * lax.rsqrt(jnp.mean(qt * qt, axis=1, keepdims=True) + EPS)
    gain = jnp.concatenate([qg_ref[...]] * (tm // LANES), axis=1)
    qt_ref[0] = (qt.reshape(d, tm) * gain).astype(BF16)

    k = _dot(xn, wk_ref[...])
    sq = (k * k).astype(BF16)
    ms = jnp.concatenate([_dot(sq[:, c:c + MXU_DIM], bd_ref[...]) for c in range(0, d, MXU_DIM)], axis=-1)
    k_ref[...] = (k * lax.rsqrt(ms + EPS) * kg_ref[...]).astype(BF16)

    vt_ref[0] = lax.dot_general(wvt_ref[...], xn, nt, preferred_element_type=F32).astype(BF16)


def _qkv_proj(h, ng, w, qg, kg, *, tm):
    n, d = h.shape
    reps = d // HEAD_DIM
    grp = jnp.arange(MXU_DIM) // HEAD_DIM
    bd = jnp.where(grp[:, None] == grp[None, :], 1.0 / HEAD_DIM, 0.0).astype(BF16)
    qg_t = jnp.tile(qg.astype(F32), reps) * (HEAD_DIM ** -0.5 * LOG2E)
    qg_t = jnp.broadcast_to(qg_t[:, None], (d, LANES))
    kg_t = jnp.tile(kg.astype(F32), reps).reshape(1, d)
    wb = w.astype(BF16)
    wqt, wk, wvt = wb[:, :d].T, wb[:, d:2 * d], wb[:, 2 * d:].T
    tok = pl.BlockSpec((tm, d), lambda i: (i, 0))
    tr = pl.BlockSpec((1, d, tm), lambda i: (i, 0, 0))
    tr_shape = jax.ShapeDtypeStruct((n // tm, d, tm), BF16)
    return pl.pallas_call(
        _qkv_kernel,
        out_shape=(tr_shape, jax.ShapeDtypeStruct((n, d), BF16), tr_shape),
        grid=(n // tm,),
        in_specs=[tok, _const_spec((1, d)), _const_spec((d, d)), _const_spec((d, d)), _const_spec((d, d)),
                  _const_spec((MXU_DIM, MXU_DIM)), _const_spec((d, LANES)), _const_spec((1, d))],
        out_specs=(tr, tok, tr),
        compiler_params=_params(("parallel",), 48),
        name="qkv_proj",
    )(h, ng.reshape(1, d).astype(F32), wqt, wk, wvt, bd, qg_t, kg_t)


def _attn_prep_kernel(rb_ref, lq1_ref, lk1_ref, lq2_ref, lk2_ref, tab_ref, lam_ref,
                      *, t, lambda_init):
    h = pl.program_id(0)
    far = rb_ref[NUM_BUCKETS - 1, h]
    blk = MAX_DISTANCE
    nb = t // blk
    skew = (lax.broadcasted_iota(jnp.int32, (blk, blk), 1) - lax.broadcasted_iota(jnp.int32, (blk, blk), 0))

    def bias_of(dd):
        n = jnp.maximum(dd, 0)
        nf = jnp.maximum(n, 1).astype(F32)
        large = MAX_EXACT + (jnp.log(nf / MAX_EXACT) / math.log(MAX_DISTANCE / MAX_EXACT)
                             * (NUM_BUCKETS - MAX_EXACT)).astype(jnp.int32)
        large = jnp.minimum(large, NUM_BUCKETS - 1)
        bucket = jnp.where(n < MAX_EXACT, n, large)
        out = jnp.zeros(dd.shape, F32)
        for bk in range(NUM_BUCKETS):
            out = jnp.where(bucket == bk, rb_ref[bk, h], out)
        return (out - far) * LOG2E

    def tile(jb, ib):
        return (slice(jb * blk, (jb + 1) * blk), slice(ib * blk, (ib + 1) * blk))

    zeros = jnp.zeros((t, t), F32)
    masked = jnp.full((blk, blk), NEG_BIG, F32)
    tab_ref[0, 0] = zeros
    for jb in range(nb):
        for ib in range(jb):
            tab_ref[(0, 0) + tile(jb, ib)] = masked
        tab_ref[(0, 0) + tile(jb, jb)] = jnp.where(skew >= 0, bias_of(skew), NEG_BIG)
        if jb + 1 < nb:
            tab_ref[(0, 0) + tile(jb, jb + 1)] = bias_of(skew + blk)
    tab_ref[0, 1] = zeros
    tab_ref[(0, 1) + tile(nb - 1, 0)] = bias_of(skew + blk)
    lam = (jnp.exp(jnp.sum(lq1_ref[...] * lk1_ref[...])) - jnp.exp(jnp.sum(lq2_ref[...] * lk2_ref[...]))
           + lambda_init)
    lam_ref[...] = jnp.full(lam_ref.shape, lam, F32)


def _attn_prep(rel_bias, lq1, lk1, lq2, lk2, *, t, lambda_init):
    row = lambda v: v.reshape(1, -1).astype(F32)
    vec = _const_spec((1, HEAD_DIM))
    tab = jax.ShapeDtypeStruct((N_HEADS, 2, t, t), F32)
    return pl.pallas_call(
        functools.partial(_attn_prep_kernel, t=t, lambda_init=lambda_init),
        out_shape=(tab, jax.ShapeDtypeStruct((8, LANES), F32)),
        grid=(N_HEADS,),
        in_specs=[pl.BlockSpec(memory_space=pltpu.SMEM), vec, vec, vec, vec],
        out_specs=(pl.BlockSpec((1, 2, t, t), lambda h: (h, 0, 0, 0)), pl.BlockSpec((8, LANES), lambda h: (0, 0))),
        compiler_params=_params(("arbitrary",), 32),
        name="attn_prep",
    )(rel_bias.astype(F32), row(lq1), row(lk1), row(lq2), row(lk2))


ONES_ROWS = 16
SUBLANES = 8


def _attn_kernel(qt_ref, k_ref, vt_ref, tab_ref, lam_ref, sg_ref, o_ref,
                 qst_ref, sa_ref, sb_ref, rma_ref, rmb_ref, m_ref, acc_ref, *, t, nq):
    r = 2 * t
    sub = lax.broadcasted_iota(jnp.int32, (V_DIM, t), 0)
    ones = jnp.ones((ONES_ROWS, t), BF16)
    assert nq >= 3

    def build_queries(qi):
        qt = qt_ref[0, qi]
        zero = jnp.zeros_like(qt)
        slot = lax.rem(qi, 2)
        qst_ref[slot, :, 0:t] = jnp.where(sub < HEAD_DIM, qt, zero)
        qst_ref[slot, :, t:r] = jnp.where(sub >= HEAD_DIM, qt, zero)

    col_blocks = [slice(j, j + MXU_DIM) for j in range(0, r, MXU_DIM)]

    def visible_keys(cols, diagonal):
        return min(t, cols.start % t + MXU_DIM) if diagonal else t

    def logits_block(qi, c, table, diagonal, cols, s_ref, rm_ref):
        nk = visible_keys(cols, diagonal)
        k = k_ref[0, pl.ds(pl.multiple_of(c * t, t), nk), :]
        s = _dot(k, qst_ref[lax.rem(qi, 2), :, cols])
        if table is not None:
            bias_cols = slice(cols.start % t, cols.start % t + MXU_DIM)
            s = s + tab_ref[0, table, 0:nk, bias_cols]
        s_ref[0:nk, cols] = s
        rm_ref[:, cols] = jnp.broadcast_to(jnp.max(s, axis=0, keepdims=True), (SUBLANES, MXU_DIM))

    def accumulate_block(qi, c, diagonal, cols, s_ref, rm_ref):
        nk = visible_keys(cols, diagonal)
        m_new = rm_ref[:, cols] if diagonal else jnp.maximum(m_ref[qi, :, cols], rm_ref[:, cols])
        p = jnp.exp2(s_ref[0:nk, cols] - jnp.tile(m_new, (nk // SUBLANES, 1))).astype(BF16)
        pv = _dot(jnp.concatenate([vt_ref[0, c, :, 0:nk], ones[:, 0:nk]], axis=0), p)
        if not diagonal:
            alpha = jnp.exp2(m_ref[qi, :, cols] - m_new)
            pv = jnp.tile(alpha, (acc_ref.shape[1] // SUBLANES, 1)) * acc_ref[qi, :, cols] + pv
        acc_ref[qi, :, cols] = pv
        m_ref[qi, :, cols] = m_new

    def step(cur, cur_diagonal, nxt, table, nxt_diagonal, buf_cur, buf_nxt):
        for cols in col_blocks:
            if nxt is not None:
                logits_block(*nxt, table, nxt_diagonal, cols, *buf_nxt)
            accumulate_block(*cur, cur_diagonal, cols, *buf_cur)

    buf_a = (sa_ref, rma_ref)
    buf_b = (sb_ref, rmb_ref)

    build_queries(0)
    build_queries(1)
    for cols in col_blocks:
        logits_block(0, 0, 0, True, cols, *buf_a)
    step((0, 0), True, (1, 1), 0, True, buf_a, buf_b)

    def near_block(qi, last):
        build_queries(2 if last else qi + 1)
        step((qi, qi), True, (qi, qi - 1), 1, False, buf_b, buf_a)
        if last:
            step((qi, qi - 1), False, (2, 0), None, False, buf_a, buf_b)
        else:
            step((qi, qi - 1), False, (qi + 1, qi + 1), 0, True, buf_a, buf_b)

    near_unroll = 7
    assert (nq - 2) % near_unroll == 0

    def near_trip(i, carry):
        for j in range(near_unroll):
            near_block(1 + near_unroll * i + j, False)
        return carry

    lax.fori_loop(0, (nq - 2) // near_unroll, near_trip, 0)
    near_block(nq - 1, True)

    def following(qi, c):
        wrap = c == 0
        qn = jnp.where(wrap, jnp.minimum(qi + 1, nq - 1), qi)
        return qn, jnp.where(wrap, qn - 2, c - 1)

    def far_step(cur, buf_cur, buf_nxt):
        qi = cur[0]
        build_queries(jnp.where(qi + 1 < nq, qi + 1, qi - 1))
        nxt = following(*cur)
        step(cur, False, nxt, None, False, buf_cur, buf_nxt)
        return nxt

    far_unroll = 26

    def far_trip(i, cur):
        for _ in range(far_unroll // 2):
            cur = far_step(far_step(cur, buf_b, buf_a), buf_a, buf_b)
        return cur

    n_far = (nq - 1) * (nq - 2) // 2
    assert n_far % far_unroll == 1
    cur = lax.fori_loop(0, n_far // far_unroll, far_trip, (jnp.int32(2), jnp.int32(0)))
    step(cur, False, None, None, False, buf_b, buf_a)

    lam = jnp.tile(lam_ref[...], (V_DIM // SUBLANES, t // LANES))
    gain = jnp.concatenate([sg_ref[...]] * (t // LANES), axis=1)

    def finish_block(qi):
        acc = acc_ref[qi]
        inv = 1.0 / acc[V_DIM:V_DIM + SUBLANES]
        o_all = acc[0:V_DIM] * jnp.tile(inv, (V_DIM // SUBLANES, 1))
        o = o_all[:, 0:t] - lam * o_all[:, t:r]
        o = o * lax.rsqrt(jnp.mean(o * o, axis=0, keepdims=True) + EPS) * gain
        o_ref[0, pl.ds(pl.multiple_of(qi * t, t), t), :] = o.T.astype(BF16)

    def finish(i, carry):
        finish_block(2 * i)
        finish_block(2 * i + 1)
        return carry

    lax.fori_loop(0, nq // 2, finish, 0)


def _attention(qt, k, vt, tab, lam, sub_g, *, t, out_scale):
    b, s, d = k.shape
    nq = s // t
    rowmajor = pl.BlockSpec((1, s, V_DIM), lambda bi, h: (bi, 0, h))
    featmajor = pl.BlockSpec((1, nq, V_DIM, t), lambda bi, h: (bi, 0, h, 0))
    return pl.pallas_call(
        functools.partial(_attn_kernel, t=t, nq=nq),
        out_shape=jax.ShapeDtypeStruct((b, s, d), BF16),
        grid=(b, N_HEADS),
        in_specs=[featmajor, rowmajor, featmajor, pl.BlockSpec((1, 2, t, t), lambda bi, h: (h, 0, 0, 0)),
                  _const_spec((8, LANES)), _const_spec((V_DIM, LANES))],
        out_specs=rowmajor,
        scratch_shapes=[pltpu.VMEM((2, V_DIM, 2 * t), BF16),
                        pltpu.VMEM((t, 2 * t), F32), pltpu.VMEM((t, 2 * t), F32),
                        pltpu.VMEM((SUBLANES, 2 * t), F32), pltpu.VMEM((SUBLANES, 2 * t), F32),
                        pltpu.VMEM((nq, SUBLANES, 2 * t), F32),
                        pltpu.VMEM((nq, V_DIM + ONES_ROWS, 2 * t), F32)],
        compiler_params=_params(("parallel", "parallel"), 58),
        name="diff_attention",
    )(qt, k, vt, tab, lam, jnp.broadcast_to((sub_g.astype(F32) * out_scale)[:, None], (V_DIM, LANES)))


def kernel(x, p, conv_norm_g, conv_w_pw1, conv_b_pw1, conv_dw_w, conv_dw_b, conv_ln_g, conv_ln_b, conv_w_pw2, conv_b_pw2, attn_norm_g, attn_w_qkv, attn_q_norm_g, attn_k_norm_g, attn_lambda_q1, attn_lambda_k1, attn_lambda_q2, attn_lambda_k2, attn_sub_norm_g, attn_w_o, rel_bias, ffn_norm_g, ffn_w_gate, ffn_w_up, ffn_w_down, ple_norm_g, ple_w_gate, ple_w_proj):
    b, s, d = x.shape
    n = b * s
    t_attn = 512

    h = _conv_mixer(x, conv_norm_g[0], conv_w_pw1[0], conv_b_pw1[0], conv_dw_w[0], conv_dw_b[0],
                    conv_ln_g[0], conv_ln_b[0], conv_w_pw2[0], conv_b_pw2[0])
    p2 = p.reshape(-1, PLE_DIM)
    h = _ffn_ple(h.reshape(n, d), p2, 0, ffn_norm_g[0], ffn_w_gate[0], ffn_w_up[0],
                 ffn_w_down[0], ple_norm_g[0], ple_w_gate[0], ple_w_proj[0])

    lambda_init = 0.8 - 0.6 * math.exp(-0.3 * 1)
    qt, k, vt = _qkv_proj(h, attn_norm_g[0], attn_w_qkv[0], attn_q_norm_g[0], attn_k_norm_g[0], tm=t_attn)
    nq = s // t_attn
    tab, lam = _attn_prep(rel_bias, attn_lambda_q1[0], attn_lambda_k1[0], attn_lambda_q2[0],
                          attn_lambda_k2[0], t=t_attn, lambda_init=lambda_init)
    on = _attention(qt.reshape(b, nq, d, t_attn), k.reshape(b, s, d), vt.reshape(b, nq, d, t_attn), tab, lam,
                    attn_sub_norm_g[0], t=t_attn, out_scale=1.0 - lambda_init)
    h = _ffn_ple(h, p2, 1, ffn_norm_g[1], ffn_w_gate[1], ffn_w_up[1], ffn_w_down[1],
                 ple_norm_g[1], ple_w_gate[1], ple_w_proj[1], pre=(on.reshape(n, d), attn_w_o[0]))
    return h.reshape(b, s, d)
```

```python
import functools
import math

import jax
import jax.numpy as jnp
from jax import lax
from jax.experimental import pallas as pl
from jax.experimental.pallas import tpu as pltpu

D_MODEL = 1024
CONV_WIDTH = 31
N_HEADS = 8
HEAD_DIM = 64
V_DIM = 128
NUM_BUCKETS = 32
MAX_EXACT = 16
MAX_DISTANCE = 128
PLE_DIM = 256
EPS = 1e-6

LANES = 128
MXU_DIM = 256
VMEM_BYTES = 64 * 2 ** 20
HALO = 32
NEG_BIG = -1e30
LOG2E = math.log2(math.e)

F32 = jnp.float32
BF16 = jnp.bfloat16


def _rms(x, g):
    return x * lax.rsqrt(jnp.mean(x * x, axis=-1, keepdims=True) + EPS) * g


def _dot(a, b):
    return jnp.dot(a, b, preferred_element_type=F32)


def _const_spec(shape):
    nd = len(shape)
    return pl.BlockSpec(shape, lambda *_: (0,) * nd, pipeline_mode=pl.Buffered(1))


def _params(semantics, vmem_mib, flags=None):
    return pltpu.CompilerParams(dimension_semantics=semantics, flags=flags,
                                vmem_limit_bytes=min(vmem_mib * 2 ** 20, VMEM_BYTES - 6 * 2 ** 20))


def _conv_mixer_kernel(x_ref, ng_ref, w1_ref, b1_ref, dw_ref, dwb_ref, lng_ref, lnb_ref,
                       w2_ref, b2_ref, o_ref, ubuf_ref, ybuf_ref, *, tm, rc):
    d = D_MODEL
    nc = d // LANES

    @pl.when(pl.program_id(1) == 0)
    def _():
        ubuf_ref[:, 0:HALO, :] = jnp.zeros((nc, HALO, LANES), F32)

    x = x_ref[0]
    xn = _rms(x, ng_ref[...]).astype(BF16)
    a = _dot(xn, w1_ref[...]) + b1_ref[...]
    u = a[:, :d] * jax.nn.sigmoid(a[:, d:])
    for c in range(nc):
        ubuf_ref[c, HALO:HALO + tm, :] = u[:, c * LANES:(c + 1) * LANES]

    def col_body(c, carry):
        for r0 in range(0, tm, rc):
            acc = None
            for k in range(CONV_WIDTH):
                seg = ubuf_ref[c, pl.ds(r0 + k + HALO - (CONV_WIDTH - 1), rc), :]
                t = seg * dw_ref[c, k:k + 1, :]
                acc = t if acc is None else acc + t
            ybuf_ref[c, r0:r0 + rc, :] = acc
        ubuf_ref[c, 0:HALO, :] = ubuf_ref[c, tm:tm + HALO, :]
        return carry

    lax.fori_loop(0, nc, col_body, 0)

    y = jnp.concatenate([ybuf_ref[c] for c in range(nc)], axis=-1) + dwb_ref[...]
    mu = jnp.mean(y, axis=-1, keepdims=True)
    yc = y - mu
    var = jnp.mean(yc * yc, axis=-1, keepdims=True)
    yn = yc * lax.rsqrt(var + EPS) * lng_ref[...] + lnb_ref[...]
    z = (yn * jax.nn.sigmoid(yn)).astype(BF16)
    o_ref[0] = x + _dot(z, w2_ref[...]) + b2_ref[...]


def _conv_mixer(x, ng, w1, b1, dw, dwb, lng, lnb, w2, b2, *, tm=1024, rc=64):
    b, s, d = x.shape
    nc = d // LANES
    dw_p = jnp.pad(dw, ((0, HALO - CONV_WIDTH), (0, 0))).reshape(HALO, nc, LANES).transpose(1, 0, 2)
    row = lambda v: v.reshape(1, -1).astype(F32)
    kern = functools.partial(_conv_mixer_kernel, tm=tm, rc=rc)
    return pl.pallas_call(
        kern,
        out_shape=jax.ShapeDtypeStruct((b, s, d), F32),
        grid=(b, s // tm),
        in_specs=[
            pl.BlockSpec((1, tm, d), lambda i, j: (i, j, 0)),
            _const_spec((1, d)), _const_spec((d, 2 * d)), _const_spec((1, 2 * d)),
            _const_spec((nc, HALO, LANES)), _const_spec((1, d)), _const_spec((1, d)),
            _const_spec((1, d)), _const_spec((d, d)), _const_spec((1, d)),
        ],
        out_specs=pl.BlockSpec((1, tm, d), lambda i, j: (i, j, 0)),
        scratch_shapes=[pltpu.VMEM((nc, HALO + tm, LANES), F32), pltpu.VMEM((nc, tm, LANES), F32)],
        compiler_params=_params(("arbitrary", "arbitrary"), 40),
        name="conv_mixer",
    )(x, row(ng), w1.astype(BF16), row(b1), dw_p, row(dwb), row(lng), row(lnb), w2.astype(BF16), row(b2))


def _ffn_chunks(d_ff):
    step = 2 * MXU_DIM
    return [(c0, min(c0 + step, d_ff)) for c0 in range(0, d_ff, step)]


def _ffn_ple_kernel(*refs, has_pre):
    if has_pre:
        (h_ref, on_ref, wo_ref, fg_ref, wg_ref, wu_ref, wd_ref, p_ref, pg_ref, wpg_ref, wpp_ref,
         o_ref) = refs
        h = h_ref[...] + _dot(on_ref[...], wo_ref[...])
    else:
        (h_ref, fg_ref, wg_ref, wu_ref, wd_ref, p_ref, pg_ref, wpg_ref, wpp_ref, o_ref) = refs
        h = h_ref[...]
    xn = _rms(h, fg_ref[...]).astype(BF16)
    acc = None
    for c0, c1 in _ffn_chunks(wg_ref.shape[1]):
        g = _dot(xn, wg_ref[:, c0:c1])
        u = _dot(xn, wu_ref[:, c0:c1])
        a = (g * jax.nn.sigmoid(g) * u).astype(BF16)
        t = _dot(a, wd_ref[c0:c1, :])
        acc = t if acc is None else acc + t
    h2 = h + acc
    gate = jax.nn.sigmoid(_dot(_rms(h2, pg_ref[...]).astype(BF16), wpg_ref[...]))
    proj = _dot(p_ref[...].astype(BF16), wpp_ref[...])
    o_ref[...] = h2 + gate * proj


def _ffn_ple(h, p, layer, fg, wg, wu, wd, pg, wpg, wpp, pre=None, *, tm=512):
    n, d = h.shape
    p_blk = pl.BlockSpec((tm, PLE_DIM), lambda i: (i + layer * (n // tm), 0))
    d_ff = wg.shape[1]
    row = lambda v: v.reshape(1, -1).astype(F32)
    tok = lambda w: pl.BlockSpec((tm, w), lambda i: (i, 0))
    args, specs = [h], [tok(d)]
    if pre is not None:
        on, wo = pre
        args += [on, wo.astype(BF16)]
        specs += [tok(d), _const_spec((d, d))]
    args += [row(fg), wg.astype(BF16), wu.astype(BF16), wd.astype(BF16), p, row(pg),
             wpg.astype(BF16), wpp.astype(BF16)]
    specs += [_const_spec((1, d)), _const_spec((d, d_ff)), _const_spec((d, d_ff)),
              _const_spec((d_ff, d)), p_blk, _const_spec((1, d)), _const_spec((d, d)),
              _const_spec((PLE_DIM, d))]
    return pl.pallas_call(
        functools.partial(_ffn_ple_kernel, has_pre=pre is not None),
        out_shape=jax.ShapeDtypeStruct((n, d), F32),
        grid=(n // tm,),
        in_specs=specs,
        out_specs=tok(d),
        compiler_params=_params(("parallel",), 56),
        name="ffn_ple_pre" if pre is not None else "ffn_ple",
    )(*args)


def _qkv_kernel(h_ref, ng_ref, wqt_ref, wk_ref, wvt_ref, bd_ref, qg_ref, kg_ref, qt_ref, k_ref, vt_ref):
    d = D_MODEL
    tm = h_ref.shape[0]
    xn = _rms(h_ref[...], ng_ref[...]).astype(BF16)
    nt = (((1,), (1,)), ((), ()))

    qt = lax.dot_general(wqt_ref[...], xn, nt, preferred_element_type=F32).reshape(d // HEAD_DIM, HEAD_DIM, tm)
    qt = qt * lax.rsqrt(jnp.mean(qt * qt, axis=1, keepdims=True) + EPS)
    gain = jnp.concatenate([qg_ref[...]] * (tm // LANES), axis=1)
    qt_ref[0] = (qt.reshape(d, tm) * gain).astype(BF16)

    k = _dot(xn, wk_ref[...])
    sq = (k * k).astype(BF16)
    ms = jnp.concatenate([_dot(sq[:, c:c + MXU_DIM], bd_ref[...]) for c in range(0, d, MXU_DIM)], axis=-1)
    k_ref[...] = (k * lax.rsqrt(ms + EPS) * kg_ref[...]).astype(BF16)

    vt_ref[0] = lax.dot_general(wvt_ref[...], xn, nt, preferred_element_type=F32).astype(BF16)


def _qkv_proj(h, ng, w, qg, kg, *, tm):
    n, d = h.shape
    reps = d // HEAD_DIM
    grp = jnp.arange(MXU_DIM) // HEAD_DIM
    bd = jnp.where(grp[:, None] == grp[None, :], 1.0 / HEAD_DIM, 0.0).astype(BF16)
    qg_t = jnp.tile(qg.astype(F32), reps) * (HEAD_DIM ** -0.5 * LOG2E)
    qg_t = jnp.broadcast_to(qg_t[:, None], (d, LANES))
    kg_t = jnp.tile(kg.astype(F32), reps).reshape(1, d)
    wb = w.astype(BF16)
    wqt, wk, wvt = wb[:, :d].T, wb[:, d:2 * d], wb[:, 2 * d:].T
    tok = pl.BlockSpec((tm, d), lambda i: (i, 0))
    tr = pl.BlockSpec((1, d, tm), lambda i: (i, 0, 0))
    tr_shape = jax.ShapeDtypeStruct((n // tm, d, tm), BF16)
    return pl.pallas_call(
        _qkv_kernel,
        out_shape=(tr_shape, jax.ShapeDtypeStruct((n, d), BF16), tr_shape),
        grid=(n // tm,),
        in_specs=[tok, _const_spec((1, d)), _const_spec((d, d)), _const_spec((d, d)), _const_spec((d, d)),
                  _const_spec((MXU_DIM, MXU_DIM)), _const_spec((d, LANES)), _const_spec((1, d))],
        out_specs=(tr, tok, tr),
        compiler_params=_params(("parallel",), 48),
        name="qkv_proj",
    )(h, ng.reshape(1, d).astype(F32), wqt, wk, wvt, bd, qg_t, kg_t)


def _attn_prep_kernel(rb_ref, lq1_ref, lk1_ref, lq2_ref, lk2_ref, tab_ref, lam_ref,
                      *, t, lambda_init):
    h = pl.program_id(0)
    far = rb_ref[NUM_BUCKETS - 1, h]
    blk = MAX_DISTANCE
    nb = t // blk
    skew = (lax.broadcasted_iota(jnp.int32, (blk, blk), 1) - lax.broadcasted_iota(jnp.int32, (blk, blk), 0))

    def bias_of(dd):
        n = jnp.maximum(dd, 0)
        nf = jnp.maximum(n, 1).astype(F32)
        large = MAX_EXACT + (jnp.log(nf / MAX_EXACT) / math.log(MAX_DISTANCE / MAX_EXACT)
                             * (NUM_BUCKETS - MAX_EXACT)).astype(jnp.int32)
        large = jnp.minimum(large, NUM_BUCKETS - 1)
        bucket = jnp.where(n < MAX_EXACT, n, large)
        out = jnp.zeros(dd.shape, F32)
        for bk in range(NUM_BUCKETS):
            out = jnp.where(bucket == bk, rb_ref[bk, h], out)
        return (out - far) * LOG2E

    def tile(jb, ib):
        return (slice(jb * blk, (jb + 1) * blk), slice(ib * blk, (ib + 1) * blk))

    zeros = jnp.zeros((t, t), F32)
    masked = jnp.full((blk, blk), NEG_BIG, F32)
    tab_ref[0, 0] = zeros
    for jb in range(nb):
        for ib in range(jb):
            tab_ref[(0, 0) + tile(jb, ib)] = masked
        tab_ref[(0, 0) + tile(jb, jb)] = jnp.where(skew >= 0, bias_of(skew), NEG_BIG)
        if jb + 1 < nb:
            tab_ref[(0, 0) + tile(jb, jb + 1)] = bias_of(skew + blk)
    tab_ref[0, 1] = zeros
    tab_ref[(0, 1) + tile(nb - 1, 0)] = bias_of(skew + blk)
    lam = (jnp.exp(jnp.sum(lq1_ref[...] * lk1_ref[...])) - jnp.exp(jnp.sum(lq2_ref[...] * lk2_ref[...]))
           + lambda_init)
    lam_ref[...] = jnp.full(lam_ref.shape, lam, F32)


def _attn_prep(rel_bias, lq1, lk1, lq2, lk2, *, t, lambda_init):
    row = lambda v: v.reshape(1, -1).astype(F32)
    vec = _const_spec((1, HEAD_DIM))
    tab = jax.ShapeDtypeStruct((N_HEADS, 2, t, t), F32)
    return pl.pallas_call(
        functools.partial(_attn_prep_kernel, t=t, lambda_init=lambda_init),
        out_shape=(tab, jax.ShapeDtypeStruct((8, LANES), F32)),
        grid=(N_HEADS,),
        in_specs=[pl.BlockSpec(memory_space=pltpu.SMEM), vec, vec, vec, vec],
        out_specs=(pl.BlockSpec((1, 2, t, t), lambda h: (h, 0, 0, 0)), pl.BlockSpec((8, LANES), lambda h: (0, 0))),
        compiler_params=_params(("arbitrary",), 32),
        name="attn_prep",
    )(rel_bias.astype(F32), row(lq1), row(lk1), row(lq2), row(lk2))


ONES_ROWS = 16
SUBLANES = 8


def _attn_kernel(qt_ref, k_ref, vt_ref, tab_ref, lam_ref, sg_ref, o_ref,
                 qst_ref, sa_ref, sb_ref, rma_ref, rmb_ref, m_ref, acc_ref, *, t, nq):
    r = 2 * t
    sub = lax.broadcasted_iota(jnp.int32, (V_DIM, t), 0)
    ones = jnp.ones((ONES_ROWS, t), BF16)
    assert nq >= 3

    def build_queries(qi):
        qt = qt_ref[0, qi]
        zero = jnp.zeros_like(qt)
        slot = lax.rem(qi, 2)
        qst_ref[slot, :, 0:t] = jnp.where(sub < HEAD_DIM, qt, zero)
        qst_ref[slot, :, t:r] = jnp.where(sub >= HEAD_DIM, qt, zero)

    col_blocks = [slice(j, j + MXU_DIM) for j in range(0, r, MXU_DIM)]

    def visible_keys(cols, diagonal):
        return min(t, cols.start % t + MXU_DIM) if diagonal else t

    def logits_block(qi, c, table, diagonal, cols, s_ref, rm_ref):
        nk = visible_keys(cols, diagonal)
        k = k_ref[0, pl.ds(pl.multiple_of(c * t, t), nk), :]
        s = _dot(k, qst_ref[lax.rem(qi, 2), :, cols])
        if table is not None:
            bias_cols = slice(cols.start % t, cols.start % t + MXU_DIM)
            s = s + tab_ref[0, table, 0:nk, bias_cols]
        s_ref[0:nk, cols] = s
        rm_ref[:, cols] = jnp.broadcast_to(jnp.max(s, axis=0, keepdims=True), (SUBLANES, MXU_DIM))

    def accumulate_block(qi, c, diagonal, cols, s_ref, rm_ref):
        nk = visible_keys(cols, diagonal)
        m_new = rm_ref[:, cols] if diagonal else jnp.maximum(m_ref[qi, :, cols], rm_ref[:, cols])
        p = jnp.exp2(s_ref[0:nk, cols] - jnp.tile(m_new, (nk // SUBLANES, 1))).astype(BF16)
        pv = _dot(jnp.concatenate([vt_ref[0, c, :, 0:nk], ones[:, 0:nk]], axis=0), p)
        if not diagonal:
            alpha = jnp.exp2(m_ref[qi, :, cols] - m_new)
            pv = jnp.tile(alpha, (acc_ref.shape[1] // SUBLANES, 1)) * acc_ref[qi, :, cols] + pv
        acc_ref[qi, :, cols] = pv
        m_ref[qi, :, cols] = m_new

    def step(cur, cur_diagonal, nxt, table, nxt_diagonal, buf_cur, buf_nxt):
        for cols in col_blocks:
            if nxt is not None:
                logits_block(*nxt, table, nxt_diagonal, cols, *buf_nxt)
            accumulate_block(*cur, cur_diagonal, cols, *buf_cur)

    buf_a = (sa_ref, rma_ref)
    buf_b = (sb_ref, rmb_ref)

    build_queries(0)
    build_queries(1)
    for cols in col_blocks:
        logits_block(0, 0, 0, True, cols, *buf_a)
    step((0, 0), True, (1, 1), 0, True, buf_a, buf_b)

    def near_block(qi, last):
        build_queries(2 if last else qi + 1)
        step((qi, qi), True, (qi, qi - 1), 1, False, buf_b, buf_a)
        if last:
            step((qi, qi - 1), False, (2, 0), None, False, buf_a, buf_b)
        else:
            step((qi, qi - 1), False, (qi + 1, qi + 1), 0, True, buf_a, buf_b)

    near_unroll = 7
    assert (nq - 2) % near_unroll == 0

    def near_trip(i, carry):
        for j in range(near_unroll):
            near_block(1 + near_unroll * i + j, False)
        return carry

    lax.fori_loop(0, (nq - 2) // near_unroll, near_trip, 0)
    near_block(nq - 1, True)

    def following(qi, c):
        wrap = c == 0
        qn = jnp.where(wrap, jnp.minimum(qi + 1, nq - 1), qi)
        return qn, jnp.where(wrap, qn - 2, c - 1)

    def far_step(cur, buf_cur, buf_nxt):
        qi = cur[0]
        build_queries(jnp.where(qi + 1 < nq, qi + 1, qi - 1))
        nxt = following(*cur)
        step(cur, False, nxt, None, False, buf_cur, buf_nxt)
        return nxt

    far_unroll = 26

    def far_trip(i, cur):
        for _ in range(far_unroll // 2):
            cur = far_step(far_step(cur, buf_b, buf_a), buf_a, buf_b)
        return cur

    n_far = (nq - 1) * (nq - 2) // 2
    assert n_far % far_unroll == 1
    cur = lax.fori_loop(0, n_far // far_unroll, far_trip, (jnp.int32(2), jnp.int32(0)))
    step(cur, False, None, None, False, buf_b, buf_a)

    lam = jnp.tile(lam_ref[...], (V_DIM // SUBLANES, t // LANES))
    gain = jnp.concatenate([sg_ref[...]] * (t // LANES), axis=1)

    def finish_block(qi):
        acc = acc_ref[qi]
        inv = 1.0 / acc[V_DIM:V_DIM + SUBLANES]
        o_all = acc[0:V_DIM] * jnp.tile(inv, (V_DIM // SUBLANES, 1))
        o = o_all[:, 0:t] - lam * o_all[:, t:r]
        o = o * lax.rsqrt(jnp.mean(o * o, axis=0, keepdims=True) + EPS) * gain
        o_ref[0, pl.ds(pl.multiple_of(qi * t, t), t), :] = o.T.astype(BF16)

    def finish(i, carry):
        finish_block(2 * i)
        finish_block(2 * i + 1)
        return carry

    lax.fori_loop(0, nq // 2, finish, 0)


def _attention(qt, k, vt, tab, lam, sub_g, *, t, out_scale):
    b, s, d = k.shape
    nq = s // t
    rowmajor = pl.BlockSpec((1, s, V_DIM), lambda bi, h: (bi, 0, h))
    featmajor = pl.BlockSpec((1, nq, V_DIM, t), lambda bi, h: (bi, 0, h, 0))
    return pl.pallas_call(
        functools.partial(_attn_kernel, t=t, nq=nq),
        out_shape=jax.ShapeDtypeStruct((b, s, d), BF16),
        grid=(b, N_HEADS),
        in_specs=[featmajor, rowmajor, featmajor, pl.BlockSpec((1, 2, t, t), lambda bi, h: (h, 0, 0, 0)),
                  _const_spec((8, LANES)), _const_spec((V_DIM, LANES))],
        out_specs=rowmajor,
        scratch_shapes=[pltpu.VMEM((2, V_DIM, 2 * t), BF16),
                        pltpu.VMEM((t, 2 * t), F32), pltpu.VMEM((t, 2 * t), F32),
                        pltpu.VMEM((SUBLANES, 2 * t), F32), pltpu.VMEM((SUBLANES, 2 * t), F32),
                        pltpu.VMEM((nq, SUBLANES, 2 * t), F32),
                        pltpu.VMEM((nq, V_DIM + ONES_ROWS, 2 * t), F32)],
        compiler_params=_params(("parallel", "parallel"), 58),
        name="diff_attention",
    )(qt, k, vt, tab, lam, jnp.broadcast_to((sub_g.astype(F32) * out_scale)[:, None], (V_DIM, LANES)))


def kernel(x, p, conv_norm_g, conv_w_pw1, conv_b_pw1, conv_dw_w, conv_dw_b, conv_ln_g, conv_ln_b, conv_w_pw2, conv_b_pw2, attn_norm_g, attn_w_qkv, attn_q_norm_g, attn_k_norm_g, attn_lambda_q1, attn_lambda_k1, attn_lambda_q2, attn_lambda_k2, attn_sub_norm_g, attn_w_o, rel_bias, ffn_norm_g, ffn_w_gate, ffn_w_up, ffn_w_down, ple_norm_g, ple_w_gate, ple_w_proj):
    b, s, d = x.shape
    n = b * s
    t_attn = 512

    h = _conv_mixer(x, conv_norm_g[0], conv_w_pw1[0], conv_b_pw1[0], conv_dw_w[0], conv_dw_b[0],
                    conv_ln_g[0], conv_ln_b[0], conv_w_pw2[0], conv_b_pw2[0])
    p2 = p.reshape(-1, PLE_DIM)
    h = _ffn_ple(h.reshape(n, d), p2, 0, ffn_norm_g[0], ffn_w_gate[0], ffn_w_up[0],
                 ffn_w_down[0], ple_norm_g[0], ple_w_gate[0], ple_w_proj[0])

    lambda_init = 0.8 - 0.6 * math.exp(-0.3 * 1)
    qt, k, vt = _qkv_proj(h, attn_norm_g[0], attn_w_qkv[0], attn_q_norm_g[0], attn_k_norm_g[0], tm=t_attn)
    nq = s // t_attn
    tab, lam = _attn_prep(rel_bias, attn_lambda_q1[0], attn_lambda_k1[0], attn_lambda_q2[0],
                          attn_lambda_k2[0], t=t_attn, lambda_init=lambda_init)
    on = _attention(qt.reshape(b, nq, d, t_attn), k.reshape(b, s, d), vt.reshape(b, nq, d, t_attn), tab, lam,
                    attn_sub_norm_g[0], t=t_attn, out_scale=1.0 - lambda_init)
    h = _ffn_ple(h, p2, 1, ffn_norm_g[1], ffn_w_gate[1], ffn_w_up[1], ffn_w_down[1],
                 ple_norm_g[1], ple_w_gate[1], ple_w_proj[1], pre=(on.reshape(n, d), attn_w_o[0]))
    return h.reshape(b, s, d)
```

```python
import functools
import math

import jax
import jax.numpy as jnp
from jax import lax
from jax.experimental import pallas as pl
from jax.experimental.pallas import tpu as pltpu

D_MODEL = 1024
CONV_WIDTH = 31
N_HEADS = 8
HEAD_DIM = 64
V_DIM = 128
NUM_BUCKETS = 32
MAX_EXACT = 16
MAX_DISTANCE = 128
PLE_DIM = 256
EPS = 1e-6

LANES = 128
MXU_DIM = 256
VMEM_BYTES = 64 * 2 ** 20
HALO = 32
NEG_BIG = -1e30
LOG2E = math.log2(math.e)

F32 = jnp.float32
BF16 = jnp.bfloat16


def _rms(x, g):
    return x * lax.rsqrt(jnp.mean(x * x, axis=-1, keepdims=True) + EPS) * g


def _dot(a, b):
    return jnp.dot(a, b, preferred_element_type=F32)


def _const_spec(shape):
    nd = len(shape)
    return pl.BlockSpec(shape, lambda *_: (0,) * nd, pipeline_mode=pl.Buffered(1))


def _params(semantics, vmem_mib, flags=None):
    return pltpu.CompilerParams(dimension_semantics=semantics, flags=flags,
                                vmem_limit_bytes=min(vmem_mib * 2 ** 20, VMEM_BYTES - 6 * 2 ** 20))


def _conv_mixer_kernel(x_ref, ng_ref, w1_ref, b1_ref, dw_ref, dwb_ref, lng_ref, lnb_ref,
                       w2_ref, b2_ref, o_ref, ubuf_ref, ybuf_ref, *, tm, rc):
    d = D_MODEL
    nc = d // LANES

    @pl.when(pl.program_id(1) == 0)
    def _():
        ubuf_ref[:, 0:HALO, :] = jnp.zeros((nc, HALO, LANES), F32)

    x = x_ref[0]
    xn = _rms(x, ng_ref[...]).astype(BF16)
    a = _dot(xn, w1_ref[...]) + b1_ref[...]
    u = a[:, :d] * jax.nn.sigmoid(a[:, d:])
    for c in range(nc):
        ubuf_ref[c, HALO:HALO + tm, :] = u[:, c * LANES:(c + 1) * LANES]

    def col_body(c, carry):
        for r0 in range(0, tm, rc):
            acc = None
            for k in range(CONV_WIDTH):
                seg = ubuf_ref[c, pl.ds(r0 + k + HALO - (CONV_WIDTH - 1), rc), :]
                t = seg * dw_ref[c, k:k + 1, :]
                acc = t if acc is None else acc + t
            ybuf_ref[c, r0:r0 + rc, :] = acc
        ubuf_ref[c, 0:HALO, :] = ubuf_ref[c, tm:tm + HALO, :]
        return carry

    lax.fori_loop(0, nc, col_body, 0)

    y = jnp.concatenate([ybuf_ref[c] for c in range(nc)], axis=-1) + dwb_ref[...]
    mu = jnp.mean(y, axis=-1, keepdims=True)
    yc = y - mu
    var = jnp.mean(yc * yc, axis=-1, keepdims=True)
    yn = yc * lax.rsqrt(var + EPS) * lng_ref[...] + lnb_ref[...]
    z = (yn * jax.nn.sigmoid(yn)).astype(BF16)
    o_ref[0] = x + _dot(z, w2_ref[...]) + b2_ref[...]


def _conv_mixer(x, ng, w1, b1, dw, dwb, lng, lnb, w2, b2, *, tm=1024, rc=64):
    b, s, d = x.shape
    nc = d // LANES
    dw_p = jnp.pad(dw, ((0, HALO - CONV_WIDTH), (0, 0))).reshape(HALO, nc, LANES).transpose(1, 0, 2)
    row = lambda v: v.reshape(1, -1).astype(F32)
    kern = functools.partial(_conv_mixer_kernel, tm=tm, rc=rc)
    return pl.pallas_call(
        kern,
        out_shape=jax.ShapeDtypeStruct((b, s, d), F32),
        grid=(b, s // tm),
        in_specs=[
            pl.BlockSpec((1, tm, d), lambda i, j: (i, j, 0)),
            _const_spec((1, d)), _const_spec((d, 2 * d)), _const_spec((1, 2 * d)),
            _const_spec((nc, HALO, LANES)), _const_spec((1, d)), _const_spec((1, d)),
            _const_spec((1, d)), _const_spec((d, d)), _const_spec((1, d)),
        ],
        out_specs=pl.BlockSpec((1, tm, d), lambda i, j: (i, j, 0)),
        scratch_shapes=[pltpu.VMEM((nc, HALO + tm, LANES), F32), pltpu.VMEM((nc, tm, LANES), F32)],
        compiler_params=_params(("arbitrary", "arbitrary"), 40),
        name="conv_mixer",
    )(x, row(ng), w1.astype(BF16), row(b1), dw_p, row(dwb), row(lng), row(lnb), w2.astype(BF16), row(b2))


def _ffn_chunks(d_ff):
    step = d_ff
    return [(c0, min(c0 + step, d_ff)) for c0 in range(0, d_ff, step)]


def _ffn_ple_kernel(*refs, has_pre):
    if has_pre:
        (h_ref, on_ref, wo_ref, fg_ref, wg_ref, wu_ref, wd_ref, p_ref, pg_ref, wpg_ref, wpp_ref,
         o_ref) = refs
        h = h_ref[...] + _dot(on_ref[...], wo_ref[...])
    else:
        (h_ref, fg_ref, wg_ref, wu_ref, wd_ref, p_ref, pg_ref, wpg_ref, wpp_ref, o_ref) = refs
        h = h_ref[...]
    xn = _rms(h, fg_ref[...]).astype(BF16)
    acc = None
    for c0, c1 in _ffn_chunks(wg_ref.shape[1]):
        g = _dot(xn, wg_ref[:, c0:c1])
        u = _dot(xn, wu_ref[:, c0:c1])
        a = (g * jax.nn.sigmoid(g) * u).astype(BF16)
        t = _dot(a, wd_ref[c0:c1, :])
        acc = t if acc is None else acc + t
    h2 = h + acc
    gate = jax.nn.sigmoid(_dot(_rms(h2, pg_ref[...]).astype(BF16), wpg_ref[...]))
    proj = _dot(p_ref[...].astype(BF16), wpp_ref[...])
    o_ref[...] = h2 + gate * proj


def _ffn_ple(h, p, layer, fg, wg, wu, wd, pg, wpg, wpp, pre=None, *, tm=512):
    n, d = h.shape
    p_blk = pl.BlockSpec((tm, PLE_DIM), lambda i: (i + layer * (n // tm), 0))
    d_ff = wg.shape[1]
    row = lambda v: v.reshape(1, -1).astype(F32)
    tok = lambda w: pl.BlockSpec((tm, w), lambda i: (i, 0))
    args, specs = [h], [tok(d)]
    if pre is not None:
        on, wo = pre
        args += [on, wo.astype(BF16)]
        specs += [tok(d), _const_spec((d, d))]
    args += [row(fg), wg.astype(BF16), wu.astype(BF16), wd.astype(BF16), p, row(pg),
             wpg.astype(BF16), wpp.astype(BF16)]
    specs += [_const_spec((1, d)), _const_spec((d, d_ff)), _const_spec((d, d_ff)),
              _const_spec((d_ff, d)), p_blk, _const_spec((1, d)), _const_spec((d, d)),
              _const_spec((PLE_DIM, d))]
    return pl.pallas_call(
        functools.partial(_ffn_ple_kernel, has_pre=pre is not None),
        out_shape=jax.ShapeDtypeStruct((n, d), F32),
        grid=(n // tm,),
        in_specs=specs,
        out_specs=tok(d),
        compiler_params=_params(("parallel",), 56),
        name="ffn_ple_pre" if pre is not None else "ffn_ple",
    )(*args)


def _qkv_kernel(h_ref, ng_ref, wqt_ref, wk_ref, wvt_ref, bd_ref, qg_ref, kg_ref, qt_ref, k_ref, vt_ref):
    d = D_MODEL
    tm = h_ref.shape[0]
    xn = _rms(h_ref[...], ng_ref[...]).astype(BF16)
    nt = (((1,), (1,)), ((), ()))

    qt = lax.dot_general(wqt_ref[...], xn, nt, preferred_element_type=F32).reshape(d // HEAD_DIM, HEAD_DIM, tm)
    qt = qt * lax.rsqrt(jnp.mean(qt * qt, axis=1, keepdims=True) + EPS)
    gain = jnp.concatenate([qg_ref[...]] * (tm // LANES), axis=1)
    qt_ref[0] = (qt.reshape(d, tm) * gain).astype(BF16)

    k = _dot(xn, wk_ref[...])
    sq = (k * k).astype(BF16)
    ms = jnp.concatenate([_dot(sq[:, c:c + MXU_DIM], bd_ref[...]) for c in range(0, d, MXU_DIM)], axis=-1)
    k_ref[...] = (k * lax.rsqrt(ms + EPS) * kg_ref[...]).astype(BF16)

    vt_ref[0] = lax.dot_general(wvt_ref[...], xn, nt, preferred_element_type=F32).astype(BF16)


def _qkv_proj(h, ng, w, qg, kg, *, tm):
    n, d = h.shape
    reps = d // HEAD_DIM
    grp = jnp.arange(MXU_DIM) // HEAD_DIM
    bd = jnp.where(grp[:, None] == grp[None, :], 1.0 / HEAD_DIM, 0.0).astype(BF16)
    qg_t = jnp.tile(qg.astype(F32), reps) * (HEAD_DIM ** -0.5 * LOG2E)
    qg_t = jnp.broadcast_to(qg_t[:, None], (d, LANES))
    kg_t = jnp.tile(kg.astype(F32), reps).reshape(1, d)
    wb = w.astype(BF16)
    wqt, wk, wvt = wb[:, :d].T, wb[:, d:2 * d], wb[:, 2 * d:].T
    tok = pl.BlockSpec((tm, d), lambda i: (i, 0))
    tr = pl.BlockSpec((1, d, tm), lambda i: (i, 0, 0))
    tr_shape = jax.ShapeDtypeStruct((n // tm, d, tm), BF16)
    return pl.pallas_call(
        _qkv_kernel,
        out_shape=(tr_shape, jax.ShapeDtypeStruct((n, d), BF16), tr_shape),
        grid=(n // tm,),
        in_specs=[tok, _const_spec((1, d)), _const_spec((d, d)), _const_spec((d, d)), _const_spec((d, d)),
                  _const_spec((MXU_DIM, MXU_DIM)), _const_spec((d, LANES)), _const_spec((1, d))],
        out_specs=(tr, tok, tr),
        compiler_params=_params(("parallel",), 48),
        name="qkv_proj",
    )(h, ng.reshape(1, d).astype(F32), wqt, wk, wvt, bd, qg_t, kg_t)


def _attn_prep_kernel(rb_ref, lq1_ref, lk1_ref, lq2_ref, lk2_ref, tab_ref, lam_ref,
                      *, t, lambda_init):
    h = pl.program_id(0)
    far = rb_ref[NUM_BUCKETS - 1, h]
    blk = MAX_DISTANCE
    nb = t // blk
    skew = (lax.broadcasted_iota(jnp.int32, (blk, blk), 1) - lax.broadcasted_iota(jnp.int32, (blk, blk), 0))

    def bias_of(dd):
        n = jnp.maximum(dd, 0)
        nf = jnp.maximum(n, 1).astype(F32)
        large = MAX_EXACT + (jnp.log(nf / MAX_EXACT) / math.log(MAX_DISTANCE / MAX_EXACT)
                             * (NUM_BUCKETS - MAX_EXACT)).astype(jnp.int32)
        large = jnp.minimum(large, NUM_BUCKETS - 1)
        bucket = jnp.where(n < MAX_EXACT, n, large)
        out = jnp.zeros(dd.shape, F32)
        for bk in range(NUM_BUCKETS):
            out = jnp.where(bucket == bk, rb_ref[bk, h], out)
        return (out - far) * LOG2E

    def tile(jb, ib):
        return (slice(jb * blk, (jb + 1) * blk), slice(ib * blk, (ib + 1) * blk))

    zeros = jnp.zeros((t, t), F32)
    masked = jnp.full((blk, blk), NEG_BIG, F32)
    tab_ref[0, 0] = zeros
    for jb in range(nb):
        for ib in range(jb):
            tab_ref[(0, 0) + tile(jb, ib)] = masked
        tab_ref[(0, 0) + tile(jb, jb)] = jnp.where(skew >= 0, bias_of(skew), NEG_BIG)
        if jb + 1 < nb:
            tab_ref[(0, 0) + tile(jb, jb + 1)] = bias_of(skew + blk)
    tab_ref[0, 1] = zeros
    tab_ref[(0, 1) + tile(nb - 1, 0)] = bias_of(skew + blk)
    lam = (jnp.exp(jnp.sum(lq1_ref[...] * lk1_ref[...])) - jnp.exp(jnp.sum(lq2_ref[...] * lk2_ref[...]))
           + lambda_init)
    lam_ref[...] = jnp.full(lam_ref.shape, lam, F32)


def _attn_prep(rel_bias, lq1, lk1, lq2, lk2, *, t, lambda_init):
    row = lambda v: v.reshape(1, -1).astype(F32)
    vec = _const_spec((1, HEAD_DIM))
    tab = jax.ShapeDtypeStruct((N_HEADS, 2, t, t), F32)
    return pl.pallas_call(
        functools.partial(_attn_prep_kernel, t=t, lambda_init=lambda_init),
        out_shape=(tab, jax.ShapeDtypeStruct((8, LANES), F32)),
        grid=(N_HEADS,),
        in_specs=[pl.BlockSpec(memory_space=pltpu.SMEM), vec, vec, vec, vec],
        out_specs=(pl.BlockSpec((1, 2, t, t), lambda h: (h, 0, 0, 0)), pl.BlockSpec((8, LANES), lambda h: (0, 0))),
        compiler_params=_params(("arbitrary",), 32),
        name="attn_prep",
    )(rel_bias.astype(F32), row(lq1), row(lk1), row(lq2), row(lk2))


ONES_ROWS = 16
SUBLANES = 8


def _attn_kernel(qt_ref, k_ref, vt_ref, tab_ref, lam_ref, sg_ref, o_ref,
                 qst_ref, sa_ref, sb_ref, rma_ref, rmb_ref, m_ref, acc_ref, *, t, nq):
    r = 2 * t
    sub = lax.broadcasted_iota(jnp.int32, (V_DIM, t), 0)
    ones = jnp.ones((ONES_ROWS, t), BF16)
    assert nq >= 3

    def build_queries(qi):
        qt = qt_ref[0, qi]
        zero = jnp.zeros_like(qt)
        slot = lax.rem(qi, 2)
        qst_ref[slot, :, 0:t] = jnp.where(sub < HEAD_DIM, qt, zero)
        qst_ref[slot, :, t:r] = jnp.where(sub >= HEAD_DIM, qt, zero)

    col_blocks = [slice(j, j + MXU_DIM) for j in range(0, r, MXU_DIM)]

    def visible_keys(cols, diagonal):
        return min(t, cols.start % t + MXU_DIM) if diagonal else t

    def logits_block(qi, c, table, diagonal, cols, s_ref, rm_ref):
        nk = visible_keys(cols, diagonal)
        k = k_ref[0, pl.ds(pl.multiple_of(c * t, t), nk), :]
        s = _dot(k, qst_ref[lax.rem(qi, 2), :, cols])
        if table is not None:
            bias_cols = slice(cols.start % t, cols.start % t + MXU_DIM)
            s = s + tab_ref[0, table, 0:nk, bias_cols]
        s_ref[0:nk, cols] = s
        rm_ref[:, cols] = jnp.broadcast_to(jnp.max(s, axis=0, keepdims=True), (SUBLANES, MXU_DIM))

    def accumulate_block(qi, c, diagonal, cols, s_ref, rm_ref):
        nk = visible_keys(cols, diagonal)
        m_new = rm_ref[:, cols] if diagonal else jnp.maximum(m_ref[qi, :, cols], rm_ref[:, cols])
        p = jnp.exp2(s_ref[0:nk, cols] - jnp.tile(m_new, (nk // SUBLANES, 1))).astype(BF16)
        pv = _dot(jnp.concatenate([vt_ref[0, c, :, 0:nk], ones[:, 0:nk]], axis=0), p)
        if not diagonal:
            alpha = jnp.exp2(m_ref[qi, :, cols] - m_new)
            pv = jnp.tile(alpha, (acc_ref.shape[1] // SUBLANES, 1)) * acc_ref[qi, :, cols] + pv
        acc_ref[qi, :, cols] = pv
        m_ref[qi, :, cols] = m_new

    def step(cur, cur_diagonal, nxt, table, nxt_diagonal, buf_cur, buf_nxt):
        for cols in col_blocks:
            if nxt is not None:
                logits_block(*nxt, table, nxt_diagonal, cols, *buf_nxt)
            accumulate_block(*cur, cur_diagonal, cols, *buf_cur)

    buf_a = (sa_ref, rma_ref)
    buf_b = (sb_ref, rmb_ref)

    build_queries(0)
    build_queries(1)
    for cols in col_blocks:
        logits_block(0, 0, 0, True, cols, *buf_a)
    step((0, 0), True, (1, 1), 0, True, buf_a, buf_b)

    def near_block(qi, last):
        build_queries(2 if last else qi + 1)
        step((qi, qi), True, (qi, qi - 1), 1, False, buf_b, buf_a)
        if last:
            step((qi, qi - 1), False, (2, 0), None, False, buf_a, buf_b)
        else:
            step((qi, qi - 1), False, (qi + 1, qi + 1), 0, True, buf_a, buf_b)

    near_unroll = 7
    assert (nq - 2) % near_unroll == 0

    def near_trip(i, carry):
        for j in range(near_unroll):
            near_block(1 + near_unroll * i + j, False)
        return carry

    lax.fori_loop(0, (nq - 2) // near_unroll, near_trip, 0)
    near_block(nq - 1, True)

    def following(qi, c):
        wrap = c == 0
        qn = jnp.where(wrap, jnp.minimum(qi + 1, nq - 1), qi)
        return qn, jnp.where(wrap, qn - 2, c - 1)

    def far_step(cur, buf_cur, buf_nxt):
        qi = cur[0]
        build_queries(jnp.where(qi + 1 < nq, qi + 1, qi - 1))
        nxt = following(*cur)
        step(cur, False, nxt, None, False, buf_cur, buf_nxt)
        return nxt

    far_unroll = 26

    def far_trip(i, cur):
        for _ in range(far_unroll // 2):
            cur = far_step(far_step(cur, buf_b, buf_a), buf_a, buf_b)
        return cur

    n_far = (nq - 1) * (nq - 2) // 2
    assert n_far % far_unroll == 1
    cur = lax.fori_loop(0, n_far // far_unroll, far_trip, (jnp.int32(2), jnp.int32(0)))
    step(cur, False, None, None, False, buf_b, buf_a)

    lam = jnp.tile(lam_ref[...], (V_DIM // SUBLANES, t // LANES))
    gain = jnp.concatenate([sg_ref[...]] * (t // LANES), axis=1)

    def finish_block(qi):
        acc = acc_ref[qi]
        inv = 1.0 / acc[V_DIM:V_DIM + SUBLANES]
        o_all = acc[0:V_DIM] * jnp.tile(inv, (V_DIM // SUBLANES, 1))
        o = o_all[:, 0:t] - lam * o_all[:, t:r]
        o = o * lax.rsqrt(jnp.mean(o * o, axis=0, keepdims=True) + EPS) * gain
        o_ref[0, pl.ds(pl.multiple_of(qi * t, t), t), :] = o.T.astype(BF16)

    def finish(i, carry):
        finish_block(2 * i)
        finish_block(2 * i + 1)
        return carry

    lax.fori_loop(0, nq // 2, finish, 0)


def _attention(qt, k, vt, tab, lam, sub_g, *, t, out_scale):
    b, s, d = k.shape
    nq = s // t
    rowmajor = pl.BlockSpec((1, s, V_DIM), lambda bi, h: (bi, 0, h))
    featmajor = pl.BlockSpec((1, nq, V_DIM, t), lambda bi, h: (bi, 0, h, 0))
    return pl.pallas_call(
        functools.partial(_attn_kernel, t=t, nq=nq),
        out_shape=jax.ShapeDtypeStruct((b, s, d), BF16),
        grid=(b, N_HEADS),
        in_specs=[featmajor, rowmajor, featmajor, pl.BlockSpec((1, 2, t, t), lambda bi, h: (h, 0, 0, 0)),
                  _const_spec((8, LANES)), _const_spec((V_DIM, LANES))],
        out_specs=rowmajor,
        scratch_shapes=[pltpu.VMEM((2, V_DIM, 2 * t), BF16),
                        pltpu.VMEM((t, 2 * t), F32), pltpu.VMEM((t, 2 * t), F32),
                        pltpu.VMEM((SUBLANES, 2 * t), F32), pltpu.VMEM((SUBLANES, 2 * t), F32),
                        pltpu.VMEM((nq, SUBLANES, 2 * t), F32),
                        pltpu.VMEM((nq, V_DIM + ONES_ROWS, 2 * t), F32)],
        compiler_params=_params(("parallel", "parallel"), 58),
        name="diff_attention",
    )(qt, k, vt, tab, lam, jnp.broadcast_to((sub_g.astype(F32) * out_scale)[:, None], (V_DIM, LANES)))


def kernel(x, p, conv_norm_g, conv_w_pw1, conv_b_pw1, conv_dw_w, conv_dw_b, conv_ln_g, conv_ln_b, conv_w_pw2, conv_b_pw2, attn_norm_g, attn_w_qkv, attn_q_norm_g, attn_k_norm_g, attn_lambda_q1, attn_lambda_k1, attn_lambda_q2, attn_lambda_k2, attn_sub_norm_g, attn_w_o, rel_bias, ffn_norm_g, ffn_w_gate, ffn_w_up, ffn_w_down, ple_norm_g, ple_w_gate, ple_w_proj):
    b, s, d = x.shape
    n = b * s
    t_attn = 512

    h = _conv_mixer(x, conv_norm_g[0], conv_w_pw1[0], conv_b_pw1[0], conv_dw_w[0], conv_dw_b[0],
                    conv_ln_g[0], conv_ln_b[0], conv_w_pw2[0], conv_b_pw2[0])
    p2 = p.reshape(-1, PLE_DIM)
    h = _ffn_ple(h.reshape(n, d), p2, 0, ffn_norm_g[0], ffn_w_gate[0], ffn_w_up[0],
                 ffn_w_down[0], ple_norm_g[0], ple_w_gate[0], ple_w_proj[0])

    lambda_init = 0.8 - 0.6 * math.exp(-0.3 * 1)
    qt, k, vt = _qkv_proj(h, attn_norm_g[0], attn_w_qkv[0], attn_q_norm_g[0], attn_k_norm_g[0], tm=t_attn)
    nq = s // t_attn
    tab, lam = _attn_prep(rel_bias, attn_lambda_q1[0], attn_lambda_k1[0], attn_lambda_q2[0],
                          attn_lambda_k2[0], t=t_attn, lambda_init=lambda_init)
    on = _attention(qt.reshape(b, nq, d, t_attn), k.reshape(b, s, d), vt.reshape(b, nq, d, t_attn), tab, lam,
                    attn_sub_norm_g[0], t=t_attn, out_scale=1.0 - lambda_init)
    h = _ffn_ple(h, p2, 1, ffn_norm_g[1], ffn_w_gate[1], ffn_w_up[1], ffn_w_down[1],
                 ple_norm_g[1], ple_w_gate[1], ple_w_proj[1], pre=(on.reshape(n, d), attn_w_o[0]))
    return h.reshape(b, s, d)
```

```python
import functools
import math

import jax
import jax.numpy as jnp
from jax import lax
from jax.experimental import pallas as pl
from jax.experimental.pallas import tpu as pltpu

D_MODEL = 1024
CONV_WIDTH = 31
N_HEADS = 8
HEAD_DIM = 64
V_DIM = 128
NUM_BUCKETS = 32
MAX_EXACT = 16
MAX_DISTANCE = 128
PLE_DIM = 256
EPS = 1e-6

LANES = 128
MXU_DIM = 256
VMEM_BYTES = 64 * 2 ** 20
HALO = 32
NEG_BIG = -1e30
LOG2E = math.log2(math.e)

F32 = jnp.float32
BF16 = jnp.bfloat16


def _rms(x, g):
    return x * lax.rsqrt(jnp.mean(x * x, axis=-1, keepdims=True) + EPS) * g


def _dot(a, b):
    return jnp.dot(a, b, preferred_element_type=F32)


def _const_spec(shape):
    nd = len(shape)
    return pl.BlockSpec(shape, lambda *_: (0,) * nd, pipeline_mode=pl.Buffered(1))


def _params(semantics, vmem_mib, flags=None):
    return pltpu.CompilerParams(dimension_semantics=semantics, flags=flags,
                                vmem_limit_bytes=min(vmem_mib * 2 ** 20, VMEM_BYTES - 6 * 2 ** 20))


def _conv_mixer_kernel(x_ref, ng_ref, w1_ref, b1_ref, dw_ref, dwb_ref, lng_ref, lnb_ref,
                       w2_ref, b2_ref, o_ref, ubuf_ref, ybuf_ref, *, tm, rc):
    d = D_MODEL
    nc = d // LANES

    @pl.when(pl.program_id(1) == 0)
    def _():
        ubuf_ref[:, 0:HALO, :] = jnp.zeros((nc, HALO, LANES), F32)

    x = x_ref[0]
    xn = _rms(x, ng_ref[...]).astype(BF16)
    a = _dot(xn, w1_ref[...]) + b1_ref[...]
    u = a[:, :d] * jax.nn.sigmoid(a[:, d:])
    for c in range(nc):
        ubuf_ref[c, HALO:HALO + tm, :] = u[:, c * LANES:(c + 1) * LANES]

    def col_body(c, carry):
        for r0 in range(0, tm, rc):
            acc = None
            for k in range(CONV_WIDTH):
                seg = ubuf_ref[c, pl.ds(r0 + k + HALO - (CONV_WIDTH - 1), rc), :]
                t = seg * dw_ref[c, k:k + 1, :]
                acc = t if acc is None else acc + t
            ybuf_ref[c, r0:r0 + rc, :] = acc
        ubuf_ref[c, 0:HALO, :] = ubuf_ref[c, tm:tm + HALO, :]
        return carry

    lax.fori_loop(0, nc, col_body, 0)

    y = jnp.concatenate([ybuf_ref[c] for c in range(nc)], axis=-1) + dwb_ref[...]
    mu = jnp.mean(y, axis=-1, keepdims=True)
    yc = y - mu
    var = jnp.mean(yc * yc, axis=-1, keepdims=True)
    yn = yc * lax.rsqrt(var + EPS) * lng_ref[...] + lnb_ref[...]
    z = (yn * jax.nn.sigmoid(yn)).astype(BF16)
    o_ref[0] = x + _dot(z, w2_ref[...]) + b2_ref[...]


def _conv_mixer(x, ng, w1, b1, dw, dwb, lng, lnb, w2, b2, *, tm=1024, rc=64):
    b, s, d = x.shape
    nc = d // LANES
    dw_p = jnp.pad(dw, ((0, HALO - CONV_WIDTH), (0, 0))).reshape(HALO, nc, LANES).transpose(1, 0, 2)
    row = lambda v: v.reshape(1, -1).astype(F32)
    kern = functools.partial(_conv_mixer_kernel, tm=tm, rc=rc)
    return pl.pallas_call(
        kern,
        out_shape=jax.ShapeDtypeStruct((b, s, d), F32),
        grid=(b, s // tm),
        in_specs=[
            pl.BlockSpec((1, tm, d), lambda i, j: (i, j, 0)),
            _const_spec((1, d)), _const_spec((d, 2 * d)), _const_spec((1, 2 * d)),
            _const_spec((nc, HALO, LANES)), _const_spec((1, d)), _const_spec((1, d)),
            _const_spec((1, d)), _const_spec((d, d)), _const_spec((1, d)),
        ],
        out_specs=pl.BlockSpec((1, tm, d), lambda i, j: (i, j, 0)),
        scratch_shapes=[pltpu.VMEM((nc, HALO + tm, LANES), F32), pltpu.VMEM((nc, tm, LANES), F32)],
        compiler_params=_params(("arbitrary", "arbitrary"), 40),
        name="conv_mixer",
    )(x, row(ng), w1.astype(BF16), row(b1), dw_p, row(dwb), row(lng), row(lnb), w2.astype(BF16), row(b2))


def _ffn_chunks(d_ff):
    step = 2 * MXU_DIM
    return [(c0, min(c0 + step, d_ff)) for c0 in range(0, d_ff, step)]


def _ffn_ple_kernel(*refs, has_pre):
    if has_pre:
        (h_ref, on_ref, wo_ref, fg_ref, wgu_ref, wd_ref, p_ref, pg_ref, wpg_ref, wpp_ref,
         o_ref) = refs
        h = h_ref[...] + _dot(on_ref[...], wo_ref[...])
    else:
        (h_ref, fg_ref, wgu_ref, wd_ref, p_ref, pg_ref, wpg_ref, wpp_ref, o_ref) = refs
        h = h_ref[...]
    xn = _rms(h, fg_ref[...]).astype(BF16)
    acc = None
    for c0, c1 in _ffn_chunks(wd_ref.shape[0]):
        gu = _dot(xn, wgu_ref[:, 2 * c0:2 * c1])
        g, u = gu[:, :c1 - c0], gu[:, c1 - c0:]
        a = (g * jax.nn.sigmoid(g) * u).astype(BF16)
        t = _dot(a, wd_ref[c0:c1, :])
        acc = t if acc is None else acc + t
    h2 = h + acc
    gate = jax.nn.sigmoid(_dot(_rms(h2, pg_ref[...]).astype(BF16), wpg_ref[...]))
    proj = _dot(p_ref[...].astype(BF16), wpp_ref[...])
    o_ref[...] = h2 + gate * proj


def _ffn_ple(h, p, layer, fg, wg, wu, wd, pg, wpg, wpp, pre=None, *, tm=512):
    n, d = h.shape
    p_blk = pl.BlockSpec((tm, PLE_DIM), lambda i: (i + layer * (n // tm), 0))
    d_ff = wg.shape[1]
    row = lambda v: v.reshape(1, -1).astype(F32)
    tok = lambda w: pl.BlockSpec((tm, w), lambda i: (i, 0))
    args, specs = [h], [tok(d)]
    if pre is not None:
        on, wo = pre
        args += [on, wo.astype(BF16)]
        specs += [tok(d), _const_spec((d, d))]
    wgu = jnp.concatenate([m[:, c0:c1] for c0, c1 in _ffn_chunks(d_ff) for m in (wg, wu)], axis=1).astype(BF16)
    args += [row(fg), wgu, wd.astype(BF16), p, row(pg), wpg.astype(BF16), wpp.astype(BF16)]
    specs += [_const_spec((1, d)), _const_spec((d, 2 * d_ff)),
              _const_spec((d_ff, d)), p_blk, _const_spec((1, d)), _const_spec((d, d)),
              _const_spec((PLE_DIM, d))]
    return pl.pallas_call(
        functools.partial(_ffn_ple_kernel, has_pre=pre is not None),
        out_shape=jax.ShapeDtypeStruct((n, d), F32),
        grid=(n // tm,),
        in_specs=specs,
        out_specs=tok(d),
        compiler_params=_params(("parallel",), 56),
        name="ffn_ple_pre" if pre is not None else "ffn_ple",
    )(*args)


def _qkv_kernel(h_ref, ng_ref, wqt_ref, wk_ref, wvt_ref, bd_ref, qg_ref, kg_ref, qt_ref, k_ref, vt_ref):
    d = D_MODEL
    tm = h_ref.shape[0]
    xn = _rms(h_ref[...], ng_ref[...]).astype(BF16)
    nt = (((1,), (1,)), ((), ()))

    qt = lax.dot_general(wqt_ref[...], xn, nt, preferred_element_type=F32).reshape(d // HEAD_DIM, HEAD_DIM, tm)
    qt = qt * lax.rsqrt(jnp.mean(qt * qt, axis=1, keepdims=True) + EPS)
    gain = jnp.concatenate([qg_ref[...]] * (tm // LANES), axis=1)
    qt_ref[0] = (qt.reshape(d, tm) * gain).astype(BF16)

    k = _dot(xn, wk_ref[...])
    sq = (k * k).astype(BF16)
    ms = jnp.concatenate([_dot(sq[:, c:c + MXU_DIM], bd_ref[...]) for c in range(0, d, MXU_DIM)], axis=-1)
    k_ref[...] = (k * lax.rsqrt(ms + EPS) * kg_ref[...]).astype(BF16)

    vt_ref[0] = lax.dot_general(wvt_ref[...], xn, nt, preferred_element_type=F32).astype(BF16)


def _qkv_proj(h, ng, w, qg, kg, *, tm):
    n, d = h.shape
    reps = d // HEAD_DIM
    grp = jnp.arange(MXU_DIM) // HEAD_DIM
    bd = jnp.where(grp[:, None] == grp[None, :], 1.0 / HEAD_DIM, 0.0).astype(BF16)
    qg_t = jnp.tile(qg.astype(F32), reps) * (HEAD_DIM ** -0.5 * LOG2E)
    qg_t = jnp.broadcast_to(qg_t[:, None], (d, LANES))
    kg_t = jnp.tile(kg.astype(F32), reps).reshape(1, d)
    wb = w.astype(BF16)
    wqt, wk, wvt = wb[:, :d].T, wb[:, d:2 * d], wb[:, 2 * d:].T
    tok = pl.BlockSpec((tm, d), lambda i: (i, 0))
    tr = pl.BlockSpec((1, d, tm), lambda i: (i, 0, 0))
    tr_shape = jax.ShapeDtypeStruct((n // tm, d, tm), BF16)
    return pl.pallas_call(
        _qkv_kernel,
        out_shape=(tr_shape, jax.ShapeDtypeStruct((n, d), BF16), tr_shape),
        grid=(n // tm,),
        in_specs=[tok, _const_spec((1, d)), _const_spec((d, d)), _const_spec((d, d)), _const_spec((d, d)),
                  _const_spec((MXU_DIM, MXU_DIM)), _const_spec((d, LANES)), _const_spec((1, d))],
        out_specs=(tr, tok, tr),
        compiler_params=_params(("parallel",), 48),
        name="qkv_proj",
    )(h, ng.reshape(1, d).astype(F32), wqt, wk, wvt, bd, qg_t, kg_t)


def _attn_prep_kernel(rb_ref, lq1_ref, lk1_ref, lq2_ref, lk2_ref, tab_ref, lam_ref,
                      *, t, lambda_init):
    h = pl.program_id(0)
    far = rb_ref[NUM_BUCKETS - 1, h]
    blk = MAX_DISTANCE
    nb = t // blk
    skew = (lax.broadcasted_iota(jnp.int32, (blk, blk), 1) - lax.broadcasted_iota(jnp.int32, (blk, blk), 0))

    def bias_of(dd):
        n = jnp.maximum(dd, 0)
        nf = jnp.maximum(n, 1).astype(F32)
        large = MAX_EXACT + (jnp.log(nf / MAX_EXACT) / math.log(MAX_DISTANCE / MAX_EXACT)
                             * (NUM_BUCKETS - MAX_EXACT)).astype(jnp.int32)
        large = jnp.minimum(large, NUM_BUCKETS - 1)
        bucket = jnp.where(n < MAX_EXACT, n, large)
        out = jnp.zeros(dd.shape, F32)
        for bk in range(NUM_BUCKETS):
            out = jnp.where(bucket == bk, rb_ref[bk, h], out)
        return (out - far) * LOG2E

    def tile(jb, ib):
        return (slice(jb * blk, (jb + 1) * blk), slice(ib * blk, (ib + 1) * blk))

    zeros = jnp.zeros((t, t), F32)
    masked = jnp.full((blk, blk), NEG_BIG, F32)
    tab_ref[0, 0] = zeros
    for jb in range(nb):
        for ib in range(jb):
            tab_ref[(0, 0) + tile(jb, ib)] = masked
        tab_ref[(0, 0) + tile(jb, jb)] = jnp.where(skew >= 0, bias_of(skew), NEG_BIG)
        if jb + 1 < nb:
            tab_ref[(0, 0) + tile(jb, jb + 1)] = bias_of(skew + blk)
    tab_ref[0, 1] = zeros
    tab_ref[(0, 1) + tile(nb - 1, 0)] = bias_of(skew + blk)
    lam = (jnp.exp(jnp.sum(lq1_ref[...] * lk1_ref[...])) - jnp.exp(jnp.sum(lq2_ref[...] * lk2_ref[...]))
           + lambda_init)
    lam_ref[...] = jnp.full(lam_ref.shape, lam, F32)


def _attn_prep(rel_bias, lq1, lk1, lq2, lk2, *, t, lambda_init):
    row = lambda v: v.reshape(1, -1).astype(F32)
    vec = _const_spec((1, HEAD_DIM))
    tab = jax.ShapeDtypeStruct((N_HEADS, 2, t, t), F32)
    return pl.pallas_call(
        functools.partial(_attn_prep_kernel, t=t, lambda_init=lambda_init),
        out_shape=(tab, jax.ShapeDtypeStruct((8, LANES), F32)),
        grid=(N_HEADS,),
        in_specs=[pl.BlockSpec(memory_space=pltpu.SMEM), vec, vec, vec, vec],
        out_specs=(pl.BlockSpec((1, 2, t, t), lambda h: (h, 0, 0, 0)), pl.BlockSpec((8, LANES), lambda h: (0, 0))),
        compiler_params=_params(("arbitrary",), 32),
        name="attn_prep",
    )(rel_bias.astype(F32), row(lq1), row(lk1), row(lq2), row(lk2))


ONES_ROWS = 16
SUBLANES = 8


def _attn_kernel(qt_ref, k_ref, vt_ref, tab_ref, lam_ref, sg_ref, o_ref,
                 qst_ref, sa_ref, sb_ref, rma_ref, rmb_ref, m_ref, acc_ref, *, t, nq):
    r = 2 * t
    sub = lax.broadcasted_iota(jnp.int32, (V_DIM, t), 0)
    ones = jnp.ones((ONES_ROWS, t), BF16)
    assert nq >= 3

    def build_queries(qi):
        qt = qt_ref[0, qi]
        zero = jnp.zeros_like(qt)
        slot = lax.rem(qi, 2)
        qst_ref[slot, :, 0:t] = jnp.where(sub < HEAD_DIM, qt, zero)
        qst_ref[slot, :, t:r] = jnp.where(sub >= HEAD_DIM, qt, zero)

    col_blocks = [slice(j, j + MXU_DIM) for j in range(0, r, MXU_DIM)]

    def visible_keys(cols, diagonal):
        return min(t, cols.start % t + MXU_DIM) if diagonal else t

    def logits_block(qi, c, table, diagonal, cols, s_ref, rm_ref):
        nk = visible_keys(cols, diagonal)
        k = k_ref[0, pl.ds(pl.multiple_of(c * t, t), nk), :]
        s = _dot(k, qst_ref[lax.rem(qi, 2), :, cols])
        if table is not None:
            bias_cols = slice(cols.start % t, cols.start % t + MXU_DIM)
            s = s + tab_ref[0, table, 0:nk, bias_cols]
        s_ref[0:nk, cols] = s
        rm_ref[:, cols] = jnp.broadcast_to(jnp.max(s, axis=0, keepdims=True), (SUBLANES, MXU_DIM))

    def accumulate_block(qi, c, diagonal, cols, s_ref, rm_ref):
        nk = visible_keys(cols, diagonal)
        m_new = rm_ref[:, cols] if diagonal else jnp.maximum(m_ref[qi, :, cols], rm_ref[:, cols])
        p = jnp.exp2(s_ref[0:nk, cols] - jnp.tile(m_new, (nk // SUBLANES, 1))).astype(BF16)
        pv = _dot(jnp.concatenate([vt_ref[0, c, :, 0:nk], ones[:, 0:nk]], axis=0), p)
        if not diagonal:
            alpha = jnp.exp2(m_ref[qi, :, cols] - m_new)
            pv = jnp.tile(alpha, (acc_ref.shape[1] // SUBLANES, 1)) * acc_ref[qi, :, cols] + pv
        acc_ref[qi, :, cols] = pv
        m_ref[qi, :, cols] = m_new

    def step(cur, cur_diagonal, nxt, table, nxt_diagonal, buf_cur, buf_nxt):
        for cols in col_blocks:
            if nxt is not None:
                logits_block(*nxt, table, nxt_diagonal, cols, *buf_nxt)
            accumulate_block(*cur, cur_diagonal, cols, *buf_cur)

    buf_a = (sa_ref, rma_ref)
    buf_b = (sb_ref, rmb_ref)

    build_queries(0)
    build_queries(1)
    for cols in col_blocks:
        logits_block(0, 0, 0, True, cols, *buf_a)
    step((0, 0), True, (1, 1), 0, True, buf_a, buf_b)

    def near_block(qi, last):
        build_queries(2 if last else qi + 1)
        step((qi, qi), True, (qi, qi - 1), 1, False, buf_b, buf_a)
        if last:
            step((qi, qi - 1), False, (2, 0), None, False, buf_a, buf_b)
        else:
            step((qi, qi - 1), False, (qi + 1, qi + 1), 0, True, buf_a, buf_b)

    near_unroll = 7
    assert (nq - 2) % near_unroll == 0

    def near_trip(i, carry):
        for j in range(near_unroll):
            near_block(1 + near_unroll * i + j, False)
        return carry

    lax.fori_loop(0, (nq - 2) // near_unroll, near_trip, 0)
    near_block(nq - 1, True)

    def following(qi, c):
        wrap = c == 0
        qn = jnp.where(wrap, jnp.minimum(qi + 1, nq - 1), qi)
        return qn, jnp.where(wrap, qn - 2, c - 1)

    def far_step(cur, buf_cur, buf_nxt):
        qi = cur[0]
        build_queries(jnp.where(qi + 1 < nq, qi + 1, qi - 1))
        nxt = following(*cur)
        step(cur, False, nxt, None, False, buf_cur, buf_nxt)
        return nxt

    far_unroll = 26

    def far_trip(i, cur):
        for _ in range(far_unroll // 2):
            cur = far_step(far_step(cur, buf_b, buf_a), buf_a, buf_b)
        return cur

    n_far = (nq - 1) * (nq - 2) // 2
    assert n_far % far_unroll == 1
    cur = lax.fori_loop(0, n_far // far_unroll, far_trip, (jnp.int32(2), jnp.int32(0)))
    step(cur, False, None, None, False, buf_b, buf_a)

    lam = jnp.tile(lam_ref[...], (V_DIM // SUBLANES, t // LANES))
    gain = jnp.concatenate([sg_ref[...]] * (t // LANES), axis=1)

    def finish_block(qi):
        acc = acc_ref[qi]
        inv = 1.0 / acc[V_DIM:V_DIM + SUBLANES]
        o_all = acc[0:V_DIM] * jnp.tile(inv, (V_DIM // SUBLANES, 1))
        o = o_all[:, 0:t] - lam * o_all[:, t:r]
        o = o * lax.rsqrt(jnp.mean(o * o, axis=0, keepdims=True) + EPS) * gain
        o_ref[0, pl.ds(pl.multiple_of(qi * t, t), t), :] = o.T.astype(BF16)

    def finish(i, carry):
        finish_block(2 * i)
        finish_block(2 * i + 1)
        return carry

    lax.fori_loop(0, nq // 2, finish, 0)


def _attention(qt, k, vt, tab, lam, sub_g, *, t, out_scale):
    b, s, d = k.shape
    nq = s // t
    rowmajor = pl.BlockSpec((1, s, V_DIM), lambda bi, h: (bi, 0, h))
    featmajor = pl.BlockSpec((1, nq, V_DIM, t), lambda bi, h: (bi, 0, h, 0))
    return pl.pallas_call(
        functools.partial(_attn_kernel, t=t, nq=nq),
        out_shape=jax.ShapeDtypeStruct((b, s, d), BF16),
        grid=(b, N_HEADS),
        in_specs=[featmajor, rowmajor, featmajor, pl.BlockSpec((1, 2, t, t), lambda bi, h: (h, 0, 0, 0)),
                  _const_spec((8, LANES)), _const_spec((V_DIM, LANES))],
        out_specs=rowmajor,
        scratch_shapes=[pltpu.VMEM((2, V_DIM, 2 * t), BF16),
                        pltpu.VMEM((t, 2 * t), F32), pltpu.VMEM((t, 2 * t), F32),
                        pltpu.VMEM((SUBLANES, 2 * t), F32), pltpu.VMEM((SUBLANES, 2 * t), F32),
                        pltpu.VMEM((nq, SUBLANES, 2 * t), F32),
                        pltpu.VMEM((nq, V_DIM + ONES_ROWS, 2 * t), F32)],
        compiler_params=_params(("parallel", "parallel"), 58),
        name="diff_attention",
    )(qt, k, vt, tab, lam, jnp.broadcast_to((sub_g.astype(F32) * out_scale)[:, None], (V_DIM, LANES)))


def kernel(x, p, conv_norm_g, conv_w_pw1, conv_b_pw1, conv_dw_w, conv_dw_b, conv_ln_g, conv_ln_b, conv_w_pw2, conv_b_pw2, attn_norm_g, attn_w_qkv, attn_q_norm_g, attn_k_norm_g, attn_lambda_q1, attn_lambda_k1, attn_lambda_q2, attn_lambda_k2, attn_sub_norm_g, attn_w_o, rel_bias, ffn_norm_g, ffn_w_gate, ffn_w_up, ffn_w_down, ple_norm_g, ple_w_gate, ple_w_proj):
    b, s, d = x.shape
    n = b * s
    t_attn = 512

    h = _conv_mixer(x, conv_norm_g[0], conv_w_pw1[0], conv_b_pw1[0], conv_dw_w[0], conv_dw_b[0],
                    conv_ln_g[0], conv_ln_b[0], conv_w_pw2[0], conv_b_pw2[0])
    p2 = p.reshape(-1, PLE_DIM)
    h = _ffn_ple(h.reshape(n, d), p2, 0, ffn_norm_g[0], ffn_w_gate[0], ffn_w_up[0],
                 ffn_w_down[0], ple_norm_g[0], ple_w_gate[0], ple_w_proj[0])

    lambda_init = 0.8 - 0.6 * math.exp(-0.3 * 1)
    qt, k, vt = _qkv_proj(h, attn_norm_g[0], attn_w_qkv[0], attn_q_norm_g[0], attn_k_norm_g[0], tm=t_attn)
    nq = s // t_attn
    tab, lam = _attn_prep(rel_bias, attn_lambda_q1[0], attn_lambda_k1[0], attn_lambda_q2[0],
                          attn_lambda_k2[0], t=t_attn, lambda_init=lambda_init)
    on = _attention(qt.reshape(b, nq, d, t_attn), k.reshape(b, s, d), vt.reshape(b, nq, d, t_attn), tab, lam,
                    attn_sub_norm_g[0], t=t_attn, out_scale=1.0 - lambda_init)
    h = _ffn_ple(h, p2, 1, ffn_norm_g[1], ffn_w_gate[1], ffn_w_up[1], ffn_w_down[1],
                 ple_norm_g[1], ple_w_gate[1], ple_w_proj[1], pre=(on.reshape(n, d), attn_w_o[0]))
    return h.reshape(b, s, d)
```
